```python
import jax, jax.numpy as jnp
from jax import lax
import numpy as np

D_MODEL = 1024
BATCH = 1
SEQ = 16384
DEPTH = 2
DEC_BATCH = 8
DEC_SEQ = 32
PAST_LEN = 2048

CHUNK = 64
MIX_WIDTH = D_MODEL
RWKV_WIDTH = MIX_WIDTH // 2
HEAD_SIZE = 64
RWKV_HEADS = RWKV_WIDTH // HEAD_SIZE
DECAY_LORA = 64
ICLR_LORA = 64
GATE_LORA = 128
VRES_LORA = 32
RWKV_COLS = 3 * RWKV_WIDTH + DECAY_LORA + ICLR_LORA + GATE_LORA
CONV_CH = MIX_WIDTH - RWKV_WIDTH
CONV_WIDTH = 31
IN_COLS = RWKV_COLS + 2 * CONV_CH
D_FF = 2816
FFN_CONV_WIDTH = 3
NORM_EPS = 1e-6
LN_EPS = 1e-5
GN_EPS = 64e-5

kernel_name = 'hymba_rwkv7_conformer_convffn_adaln_step'


def _f32(t):
    return t.astype(jnp.float32)


def rms_norm(x, g):
    xf = _f32(x)
    y = xf * lax.rsqrt(jnp.mean(xf * xf, axis=-1, keepdims=True) + NORM_EPS)
    return (y * _f32(g)).astype(x.dtype)


def modulate(h, shift, scale):
    return h * (1 + scale[:, None, :]) + shift[:, None, :]


def causal_dwconv(x, past, w, b):
    k_width, ch = w.shape
    xp = jnp.concatenate([past.astype(x.dtype), x], axis=1)
    y = lax.conv_general_dilated(xp, w[:, None, :].astype(x.dtype), window_strides=(1,), padding='VALID',
                                 dimension_numbers=('NWC', 'WIO', 'NWC'), feature_group_count=ch)
    return y + b.astype(x.dtype), xp[:, xp.shape[1] - (k_width - 1):]


def wkv7_scan(r, w, k, v, kk, a, s0):
    def step(s, inp):
        r_t, w_t, k_t, v_t, kk_t, a_t = inp
        s_kk = jnp.einsum('bhvk,bhk->bhv', s, kk_t)
        s = (s * w_t[:, :, None, :] - s_kk[..., None] * (kk_t * a_t)[:, :, None, :]
             + v_t[..., None] * k_t[:, :, None, :])
        return s, jnp.einsum('bhvk,bhk->bhv', s, r_t)
    xs = tuple(jnp.swapaxes(t, 0, 1) for t in (r, w, k, v, kk, a))
    s_final, y = lax.scan(step, s0, xs)
    return jnp.swapaxes(y, 0, 1), s_final


def rwkv7_group(xs, s0, v_first, p, vres):
    bsz, seq, _ = xs.shape
    heads = lambda t: t.reshape(bsz, seq, RWKV_HEADS, HEAD_SIZE)
    xs = _f32(xs)
    splits = np.cumsum([RWKV_WIDTH, RWKV_WIDTH, RWKV_WIDTH, DECAY_LORA, ICLR_LORA]).tolist()
    xr, xk, xv, xw, xa, xg = jnp.split(xs, splits, axis=-1)
    w_log = -jax.nn.softplus(-(_f32(p['w0']) + jnp.tanh(xw) @ _f32(p['w2']))) - 0.5
    decay = jnp.exp(-jnp.exp(w_log))
    v = xv
    if vres is None:
        v_first = v
    else:
        v0, v1, v2 = vres
        v = v + (v_first - v) * jax.nn.sigmoid(_f32(v0) + (xv @ _f32(v1)) @ _f32(v2))
    a = jax.nn.sigmoid(_f32(p['a0']) + xa @ _f32(p['a2']))
    g = jax.nn.sigmoid(xg) @ _f32(p['g2'])
    kk = heads(xk * _f32(p['k_k']))
    kk = kk / jnp.maximum(jnp.sqrt(jnp.sum(kk * kk, axis=-1, keepdims=True)), 1e-12)
    k = xk * (1 + (a - 1) * _f32(p['k_a']))
    rh, kh, vh = heads(xr), heads(k), heads(v)
    y, s_new = wkv7_scan(rh, heads(decay), kh, vh, kk, heads(a), _f32(s0))
    mean = jnp.mean(y, axis=-1, keepdims=True)
    var = jnp.mean(jnp.square(y - mean), axis=-1, keepdims=True)
    yn = ((y - mean) * lax.rsqrt(var + GN_EPS)).reshape(bsz, seq, RWKV_WIDTH) * _f32(p['lnx_w']) + _f32(p['lnx_b'])
    bonus = (jnp.sum(rh * kh * _f32(p['r_k']), axis=-1, keepdims=True) * vh).reshape(bsz, seq, RWKV_WIDTH)
    return (yn + bonus) * g, s_new, v_first


def conformer_group(cv, past, p):
    glu = cv[..., :CONV_CH] * jax.nn.sigmoid(cv[..., CONV_CH:])
    y, new_past = causal_dwconv(glu, past, p['conv_w'], p['conv_b'])
    yf = _f32(y)
    mean = jnp.mean(yf, axis=-1, keepdims=True)
    var = jnp.mean(jnp.square(yf - mean), axis=-1, keepdims=True)
    yn = (yf - mean) * lax.rsqrt(var + LN_EPS) * _f32(p['cln_w']) + _f32(p['cln_b'])
    return jax.nn.silu(yn), new_past


def encoder_layer(x, c, s_wkv, s_shift, s_conv, s_ffn, v_first, p, vres):
    ada = jax.nn.silu(c) @ p['ada_w'] + p['ada_b']
    sh1, sc1, gt1, sh2, sc2, gt2 = jnp.split(ada, 6, axis=-1)
    h = modulate(rms_norm(x, p['norm1_g']), sh1, sc1)
    proj = h @ p['w_in']
    rw, cv = proj[..., :RWKV_COLS], proj[..., RWKV_COLS:]
    prev = jnp.concatenate([s_shift.astype(rw.dtype), rw[:, :-1]], axis=1)
    xs = rw + (prev - rw) * p['mu_shift']
    a_out, new_wkv, v_first = rwkv7_group(xs, s_wkv, v_first, p, vres)
    b_out, new_conv = conformer_group(cv, s_conv, p)
    mix = jnp.concatenate([a_out, b_out], axis=-1).astype(x.dtype) @ p['w_out']
    x = x + gt1[:, None, :] * mix
    h2 = modulate(rms_norm(x, p['norm2_g']), sh2, sc2)
    up, new_ffn = causal_dwconv(h2 @ p['ffn_up'], s_ffn, p['ffn_conv_w'], p['ffn_conv_b'])
    gate, val = jnp.split(up, 2, axis=-1)
    x = x + gt2[:, None, :] * ((jax.nn.silu(gate) * val) @ p['ffn_down'])
    return x, v_first, new_wkv.astype(s_wkv.dtype), rw[:, -1:], new_conv, new_ffn


def trunk(x, c, st_wkv, st_shift, st_conv, st_ffn, lp, v0, v1, v2, final_g):
    v_first = None
    acc = ([], [], [], [])
    for l in range(DEPTH):
        p = {name: t[l] for name, t in lp.items()}
        vres = None if l == 0 else (v0[l - 1], v1[l - 1], v2[l - 1])
        x, v_first, n_wkv, n_shift, n_conv, n_ffn = encoder_layer(
            x, c, st_wkv[l], st_shift[l], st_conv[l], st_ffn[l], v_first, p, vres)
        acc[0].append(n_wkv)
        acc[1].append(n_shift)
        acc[2].append(n_conv)
        acc[3].append(n_ffn)
    return rms_norm(x, final_g), [jnp.stack(a) for a in acc]


def setup_inputs(seed: int = 0) -> dict:
    key = jax.random.key(seed)
    ks = jax.random.split(key, 40)
    n = lambda i, shape, s=1.0: jax.random.normal(ks[i], shape, jnp.float32) * s
    u = lambda i, shape, lo, hi: jax.random.uniform(ks[i], shape, jnp.float32, lo, hi)
    D, RW, L = D_MODEL, RWKV_WIDTH, DEPTH
    return {
        'x_prompt': n(0, (BATCH, SEQ, D)),
        'x_sample': n(1, (DEC_BATCH, DEC_SEQ, D)),
        'c_prompt': n(2, (BATCH, D)),
        'c_sample': n(3, (DEC_BATCH, D)),
        'state_wkv': n(4, (L, DEC_BATCH, RWKV_HEADS, HEAD_SIZE, HEAD_SIZE), 0.3),
        'state_shift': n(5, (L, DEC_BATCH, 1, RWKV_COLS)),
        'state_conv': n(6, (L, DEC_BATCH, CONV_WIDTH - 1, CONV_CH), 0.5),
        'state_ffn_conv': n(7, (L, DEC_BATCH, FFN_CONV_WIDTH - 1, 2 * D_FF)),
        'norm1_g': 1.0 + n(8, (L, D), 0.02),
        'ada_w': n(9, (L, D, 6 * D), 0.5 * D ** -0.5),
        'ada_b': n(10, (L, 6 * D), 0.02),
        'w_in': n(11, (L, D, IN_COLS), D ** -0.5),
        'mu_shift': u(12, (L, RWKV_COLS), 0.0, 1.0),
        'w0': u(13, (L, RW), -3.0, 0.0),
        'w2': n(14, (L, DECAY_LORA, RW), 0.1 * DECAY_LORA ** -0.5),
        'a0': n(15, (L, RW), 0.1),
        'a2': n(16, (L, ICLR_LORA, RW), 0.1 * ICLR_LORA ** -0.5),
        'g2': n(17, (L, GATE_LORA, RW), GATE_LORA ** -0.5),
        'k_k': 0.85 + n(18, (L, RW), 0.02),
        'k_a': 1.0 + n(19, (L, RW), 0.02),
        'r_k': n(20, (L, RWKV_HEADS, HEAD_SIZE), 0.1),
        'v0': 1.0 + n(21, (L - 1, RW), 0.1),
        'v1': n(22, (L - 1, RW, VRES_LORA), RW ** -0.5),
        'v2': n(23, (L - 1, VRES_LORA, RW), 0.1 * VRES_LORA ** -0.5),
        'lnx_w': 1.0 + n(24, (L, RW), 0.02),
        'lnx_b': n(25, (L, RW), 0.02),
        'conv_w': n(26, (L, CONV_WIDTH, CONV_CH), CONV_WIDTH ** -0.5),
        'conv_b': n(27, (L, CONV_CH), 0.02),
        'cln_w': 1.0 + n(28, (L, CONV_CH), 0.02),
        'cln_b': n(29, (L, CONV_CH), 0.02),
        'w_out': n(30, (L, D, D), D ** -0.5),
        'norm2_g': 1.0 + n(31, (L, D), 0.02),
        'ffn_up': n(32, (L, D, 2 * D_FF), D ** -0.5),
        'ffn_conv_w': n(33, (L, FFN_CONV_WIDTH, 2 * D_FF), FFN_CONV_WIDTH ** -0.5),
        'ffn_conv_b': n(34, (L, 2 * D_FF), 0.02),
        'ffn_down': n(35, (L, D_FF, D), D_FF ** -0.5),
        'final_g': 1.0 + n(36, (D,), 0.02),
    }


def reference(x_prompt, x_sample, c_prompt, c_sample, state_wkv, state_shift, state_conv, state_ffn_conv,
              norm1_g, ada_w, ada_b, w_in, mu_shift, w0, w2, a0, a2, g2, k_k, k_a, r_k, v0, v1, v2,
              lnx_w, lnx_b, conv_w, conv_b, cln_w, cln_b, w_out, norm2_g, ffn_up, ffn_conv_w, ffn_conv_b,
              ffn_down, final_g):
    lp = dict(norm1_g=norm1_g, ada_w=ada_w, ada_b=ada_b, w_in=w_in, mu_shift=mu_shift, w0=w0, w2=w2,
              a0=a0, a2=a2, g2=g2, k_k=k_k, k_a=k_a, r_k=r_k, lnx_w=lnx_w, lnx_b=lnx_b, conv_w=conv_w,
              conv_b=conv_b, cln_w=cln_w, cln_b=cln_b, w_out=w_out, norm2_g=norm2_g, ffn_up=ffn_up,
              ffn_conv_w=ffn_conv_w, ffn_conv_b=ffn_conv_b, ffn_down=ffn_down)
    bp = x_prompt.shape[0]
    dt = x_prompt.dtype
    z_wkv = jnp.zeros((DEPTH, bp, RWKV_HEADS, HEAD_SIZE, HEAD_SIZE), state_wkv.dtype)
    z_shift = jnp.zeros((DEPTH, bp, 1, RWKV_COLS), dt)
    z_conv = jnp.zeros((DEPTH, bp, CONV_WIDTH - 1, CONV_CH), dt)
    z_ffn = jnp.zeros((DEPTH, bp, FFN_CONV_WIDTH - 1, 2 * D_FF), dt)
    y_prompt, (wkv_p, shift_p, conv_p, ffn_p) = trunk(
        x_prompt, c_prompt, z_wkv, z_shift, z_conv, z_ffn, lp, v0, v1, v2, final_g)
    y_sample, (wkv_s, shift_s, conv_s, ffn_s) = trunk(
        x_sample, c_sample, state_wkv, state_shift, state_conv, state_ffn_conv, lp, v0, v1, v2, final_g)
    return (y_prompt, y_sample, wkv_p, shift_p, conv_p, ffn_p, wkv_s, shift_s, conv_s, ffn_s)
```

```python
import functools

import jax
import jax.numpy as jnp
from jax import lax
from jax.experimental import pallas as pl
from jax.experimental.pallas import tpu as pltpu

D_MODEL = 1024
DEPTH = 2
RWKV_WIDTH = 512
HEAD_SIZE = 64
RWKV_HEADS = RWKV_WIDTH // HEAD_SIZE
DECAY_LORA = 64
ICLR_LORA = 64
GATE_LORA = 128
RWKV_COLS = 3 * RWKV_WIDTH + DECAY_LORA + ICLR_LORA + GATE_LORA
CONV_CH = 512
CONV_WIDTH = 31
IN_COLS = RWKV_COLS + 2 * CONV_CH
D_FF = 2816
FFN_CONV_WIDTH = 3
NORM_EPS = 1e-6
LN_EPS = 1e-5
GN_EPS = 64e-5

MXU_WIDTH = 256
HEADS_PER_GROUP = MXU_WIDTH // HEAD_SIZE
HEAD_GROUPS = RWKV_HEADS // HEADS_PER_GROUP
FFN_SLAB = MXU_WIDTH
FFN_SLABS = D_FF // FFN_SLAB
MAX_CHUNK = 64
MAX_TIME_TILE = 512
CONV_HALO = 32
CONV_ROW_BLOCK = 64
VMEM_LIMIT = 56 * 1024 * 1024

F32 = jnp.float32
BF16 = jnp.bfloat16


def _dot(a, b):
    return jnp.dot(a, b, preferred_element_type=F32)


def _dot_nt(a, b):
    return lax.dot_general(a, b, (((1,), (1,)), ((), ())), preferred_element_type=F32)


def _split2(x):
    hi = x.astype(BF16)
    lo = (x - hi.astype(F32)).astype(BF16)
    return hi, lo


def _split3(x):
    hi = x.astype(BF16)
    rem = x - hi.astype(F32)
    mid = rem.astype(BF16)
    lo = (rem - mid.astype(F32)).astype(BF16)
    return hi, mid, lo


def _mm3(a, b_parts, dot=_dot):
    a_hi, a_lo = _split2(a)
    b_hi, b_lo = b_parts
    return dot(a_hi, b_hi) + dot(a_hi, b_lo) + dot(a_lo, b_hi)


def _head_sum(x, seg):
    hi, lo = _split2(x)
    return _dot(hi, seg) + _dot(lo, seg)


def _log2(n):
    assert n & (n - 1) == 0
    return n.bit_length() - 1


def _div(x, n):
    return x >> _log2(n)


def _mod(x, n):
    return x & (n - 1)


def _head_seg_matrix(width):
    row = _div(lax.broadcasted_iota(jnp.int32, (width, width), 0), HEAD_SIZE)
    col = _div(lax.broadcasted_iota(jnp.int32, (width, width), 1), HEAD_SIZE)
    return (row == col).astype(BF16)


def _sigmoid(x):
    return 1.0 / (1.0 + jnp.exp(-x))


def _silu(x):
    return x * _sigmoid(x)


def _rms_norm(x, g):
    return x * lax.rsqrt(jnp.mean(x * x, axis=-1, keepdims=True) + NORM_EPS) * g


def _shift_rows(x, carry_rows, shift):
    rows = lax.broadcasted_iota(jnp.int32, x.shape, 0)
    out = pltpu.roll(x, shift, 0)
    n_carry = carry_rows.shape[0]
    for i in range(shift):
        out = jnp.where(rows == i, carry_rows[n_carry - shift + i:n_carry - shift + i + 1], out)
    return out


def _ada_kernel(c_ref, w_ref, b_ref, o_ref):
    c = c_ref[...]
    o_ref[0] = _dot(_silu(c).astype(BF16), w_ref[0].astype(BF16)) + b_ref[0]


def _ada_call(c_all, ada_w, ada_b):
    rows = c_all.shape[0]
    n_tile = 1536
    return pl.pallas_call(
        _ada_kernel,
        grid=(DEPTH, 6 * D_MODEL // n_tile),
        in_specs=[
            pl.BlockSpec((rows, D_MODEL), lambda l, n: (0, 0)),
            pl.BlockSpec((1, D_MODEL, n_tile), lambda l, n: (l, 0, n)),
            pl.BlockSpec((1, 1, n_tile), lambda l, n: (l, 0, n)),
        ],
        out_specs=pl.BlockSpec((1, rows, n_tile), lambda l, n: (l, 0, n)),
        out_shape=jax.ShapeDtypeStruct((DEPTH, rows, 6 * D_MODEL), F32),
        compiler_params=pltpu.CompilerParams(
            dimension_semantics=("arbitrary", "arbitrary"), vmem_limit_bytes=VMEM_LIMIT),
        name="ada_vectors",
    )(c_all, ada_w, ada_b.reshape(DEPTH, 1, 6 * D_MODEL))


def _mix_in_kernel(has_vres, tm, *refs):
    (x_ref, ada_ref, g1_ref, win_ref, sh0_ref, mu_ref, w0_ref, lora_ref, a0_ref, g2_ref,
     kk_ref, ka_ref) = refs[:12]
    refs = refs[12:]
    if has_vres:
        v0_ref, v1_ref, v2_ref, vf_ref = refs[:4]
        refs = refs[4:]
    r_o, lw_o, k_o, v_o, kkn_o, a_o, g_o, glu_o, shift_o, carry = refs

    @pl.when(pl.program_id(1) == 0)
    def _():
        carry[...] = sh0_ref[0]

    x = x_ref[0]
    ada = ada_ref[0]
    h = _rms_norm(x, g1_ref[...]) * (1.0 + ada[1:2]) + ada[0:1]
    proj = _dot(h.astype(BF16), win_ref[...])
    rw = proj[:, :RWKV_COLS]
    cv = proj[:, RWKV_COLS:]
    glu_o[0] = cv[:, :CONV_CH] * _sigmoid(cv[:, CONV_CH:])

    last = rw[tm - 1:tm]
    shift_o[0] = last
    prev = _shift_rows(rw, carry[...], 1)
    carry[...] = last
    xs = rw + (prev - rw) * mu_ref[...]

    w = RWKV_WIDTH
    xr, xk, xv = xs[:, :w], xs[:, w:2 * w], xs[:, 2 * w:3 * w]
    x_lora = xs[:, 3 * w:3 * w + DECAY_LORA + ICLR_LORA]
    xg = xs[:, 3 * w + DECAY_LORA + ICLR_LORA:]
    lane = lax.broadcasted_iota(jnp.int32, x_lora.shape, 1)
    lora_in = jnp.where(lane < DECAY_LORA, jnp.tanh(x_lora), x_lora)
    lora = _dot(lora_in.astype(BF16), lora_ref[...])

    z = -(w0_ref[...] + lora[:, :w])
    softplus = jnp.maximum(z, 0.0) + jnp.log(1.0 + jnp.exp(-jnp.abs(z)))
    lw_o[0] = -jnp.exp(-softplus - 0.5)
    a = _sigmoid(a0_ref[...] + lora[:, w:])
    a_o[0] = a
    g_o[0] = _dot(_sigmoid(xg).astype(BF16), g2_ref[...])

    if has_vres:
        gate = _dot(_dot(xv.astype(BF16), v1_ref[...]).astype(BF16), v2_ref[...])
        v_o[0] = xv + (vf_ref[0] - xv) * _sigmoid(v0_ref[...] + gate)
    else:
        v_o[0] = xv

    kk = xk * kk_ref[...]
    norm = jnp.sqrt(_head_sum(kk * kk, _head_seg_matrix(w)))
    kkn_o[0] = kk / jnp.maximum(norm, 1e-12)
    k_o[0] = xk * (1.0 + (a - 1.0) * ka_ref[...])
    r_o[0] = xr


def _const_spec(shape):
    return pl.BlockSpec(shape, lambda b, t: (0,) * len(shape), pipeline_mode=pl.Buffered(1))


def _mix_in_call(x, ada, shift0, p, vres, v_first, tm):
    bsz, seq, _ = x.shape
    w = RWKV_WIDTH
    tile = lambda width: pl.BlockSpec((1, tm, width), lambda b, t: (b, t, 0))
    per_batch = lambda rows, width: pl.BlockSpec((1, rows, width), lambda b, t: (b, 0, 0))
    args = [x, ada, p["norm1_g"], p["w_in"], shift0, p["mu_shift"], p["w0"], p["lora"], p["a0"],
            p["g2"], p["k_k"], p["k_a"]]
    in_specs = [tile(D_MODEL), per_batch(6, D_MODEL), _const_spec((1, D_MODEL)),
                _const_spec((D_MODEL, IN_COLS)), per_batch(1, RWKV_COLS), _const_spec((1, RWKV_COLS)),
                _const_spec((1, w)), _const_spec((DECAY_LORA + ICLR_LORA, 2 * w)), _const_spec((1, w)),
                _const_spec((GATE_LORA, w)), _const_spec((1, w)), _const_spec((1, w))]
    if vres is not None:
        v0, v1, v2 = vres
        args += [v0, v1, v2, v_first]
        in_specs += [_const_spec((1, w)), _const_spec(v1.shape), _const_spec(v2.shape), tile(w)]
    act = jax.ShapeDtypeStruct((bsz, seq, w), F32)
    return pl.pallas_call(
        functools.partial(_mix_in_kernel, vres is not None, tm),
        grid=(bsz, seq // tm),
        in_specs=in_specs,
        out_specs=[tile(w)] * 8 + [per_batch(1, RWKV_COLS)],
        out_shape=[act] * 8 + [jax.ShapeDtypeStruct((bsz, 1, RWKV_COLS), F32)],
        scratch_shapes=[pltpu.VMEM((1, RWKV_COLS), F32)],
        compiler_params=pltpu.CompilerParams(
            dimension_semantics=("arbitrary", "arbitrary"), vmem_limit_bytes=VMEM_LIMIT),
        name="mix_in",
    )(*args)


def _wkv_kernel(tm, chunk, r_ref, lw_ref, k_ref, v_ref, kk_ref, a_ref, g_ref, s0_ref, rk_ref,
                lnw_ref, lnb_ref, out_ref, sn_ref, state):
    n_heads = HEADS_PER_GROUP
    gw = MXU_WIDTH
    side = n_heads * chunk

    @pl.when(pl.program_id(1) == 0)
    def _():
        state[...] = s0_ref[0]

    def iota(shape, dim):
        return lax.broadcasted_iota(jnp.int32, shape, dim)

    mask_data = _div(iota((side, gw), 0), chunk) == _div(iota((side, gw), 1), HEAD_SIZE)
    mask_side = _div(iota((side, side), 0), chunk) == _div(iota((side, side), 1), chunk)
    mask_state = _div(iota((gw, gw), 0), HEAD_SIZE) == _div(iota((gw, gw), 1), HEAD_SIZE)
    t_idx = iota((chunk, side), 0)
    s_idx = _mod(iota((chunk, side), 1), chunk)
    strict = s_idx < t_idx
    incl = s_idx <= t_idx
    eye = (s_idx == t_idx).astype(F32)
    tril = (iota((chunk, chunk), 1) <= iota((chunk, chunk), 0)).astype(BF16)
    seg = _head_seg_matrix(RWKV_WIDTH)

    def expand(parts, mask):
        zero = jnp.zeros((), BF16)
        return tuple(jnp.where(mask, jnp.concatenate([q] * n_heads, axis=0), zero) for q in parts)

    def expand_data(x):
        return expand(_split2(x), mask_data)

    def expand_side(x):
        return expand(_split2(x), mask_side)

    def chunk_step(c, carry):
        rows = pl.ds(pl.multiple_of(c * chunk, chunk), chunk)
        r = r_ref[0, rows, :]
        lw = lw_ref[0, rows, :]
        k = k_ref[0, rows, :]
        v = v_ref[0, rows, :]
        kk = kk_ref[0, rows, :]
        a = a_ref[0, rows, :]

        l_hi, l_mid, l_lo = _split3(lw)
        cum = _dot(tril, l_hi) + _dot(tril, l_mid) + _dot(tril, l_lo)
        total = cum[chunk - 1:chunk]
        decay_in = jnp.exp(cum)
        decay_in_prev = jnp.exp(cum - lw)
        decay_inv = jnp.exp(-cum)
        decay_out = jnp.exp(total - cum)
        decay_all = jnp.exp(total)

        bhat = kk * a
        qa = jnp.concatenate([-kk * decay_in_prev, r * decay_in], axis=0)
        kb_b = bhat * decay_inv
        kb_k = k * decay_inv
        bdec = bhat * decay_out
        kdec = k * decay_out

        for grp in range(HEAD_GROUPS):
            lanes = slice(grp * gw, (grp + 1) * gw)
            qa_g = qa[:, lanes]
            v_g = v[:, lanes]
            a_b = _mm3(qa_g, expand_data(kb_b[:, lanes]), _dot_nt)
            a_k = _mm3(qa_g, expand_data(kb_k[:, lanes]), _dot_nt)
            a_ab = jnp.where(strict, a_b[:chunk], 0.0)
            a_rb = jnp.where(incl, a_b[chunk:], 0.0)
            a_ak = jnp.where(strict, a_k[:chunk], 0.0)
            a_rk = jnp.where(incl, a_k[chunk:], 0.0)

            def lower_left(blk):
                return ((_div(t_idx, 2 * blk) == _div(s_idx, 2 * blk))
                        & (_mod(_div(t_idx, blk), 2) == 1) & (_mod(_div(s_idx, blk), 2) == 0))

            tinv = eye + jnp.where(lower_left(1), a_ab, 0.0)
            blk = 2
            while blk < chunk:
                inner = _mm3(jnp.where(lower_left(blk), a_ab, 0.0), expand_side(tinv))
                tinv = tinv + _mm3(tinv, expand_side(inner))
                blk *= 2

            s_g = state[grp]
            x = _mm3(qa_g, _split2(s_g), _dot_nt)
            v_exp = expand_data(v_g)
            z = x[:chunk] + _mm3(a_ak, v_exp)
            u = _mm3(tinv, expand_data(z))
            y = x[chunk:] + _mm3(a_rb, expand_data(u)) + _mm3(a_rk, v_exp)
            upd = _mm3(u.T, _split2(bdec[:, lanes])) + _mm3(v_g.T, _split2(kdec[:, lanes]))
            state[grp] = s_g * decay_all[:, lanes] + jnp.where(mask_state, upd, 0.0)
            out_ref[0, rows, lanes] = y

        y = out_ref[0, rows, :]
        inv_n = 1.0 / HEAD_SIZE
        cen = y - _head_sum(y, seg) * inv_n
        var = _head_sum(cen * cen, seg) * inv_n
        yn = cen * lax.rsqrt(var + GN_EPS) * lnw_ref[...] + lnb_ref[...]
        bonus = _head_sum(r * k * rk_ref[...], seg) * v
        out_ref[0, rows, :] = (yn + bonus) * g_ref[0, rows, :]
        return carry

    lax.fori_loop(0, tm // chunk, chunk_step, 0)
    sn_ref[0] = state[...]


def _wkv_call(r, lw, k, v, kk, a, g, s0_bd, p, tm, chunk):
    bsz, seq, w = r.shape
    tile = pl.BlockSpec((1, tm, w), lambda b, t: (b, t, 0))
    st = pl.BlockSpec((1, HEAD_GROUPS, MXU_WIDTH, MXU_WIDTH), lambda b, t: (b, 0, 0, 0))
    vec = _const_spec((1, w))
    return pl.pallas_call(
        functools.partial(_wkv_kernel, tm, chunk),
        grid=(bsz, seq // tm),
        in_specs=[tile] * 7 + [st, vec, vec, vec],
        out_specs=[tile, st],
        out_shape=[jax.ShapeDtypeStruct((bsz, seq, w), F32),
                   jax.ShapeDtypeStruct((bsz, HEAD_GROUPS, MXU_WIDTH, MXU_WIDTH), F32)],
        scratch_shapes=[pltpu.VMEM((HEAD_GROUPS, MXU_WIDTH, MXU_WIDTH), F32)],
        compiler_params=pltpu.CompilerParams(
            dimension_semantics=("arbitrary", "arbitrary"), vmem_limit_bytes=VMEM_LIMIT),
        name="wkv7_chunked",
    )(r, lw, k, v, kk, a, g, s0_bd, p["r_k"], p["lnx_w"], p["lnx_b"])


def _state_to_blockdiag(s):
    bsz = s.shape[0]
    s = s.reshape(bsz, HEAD_GROUPS, HEADS_PER_GROUP, HEAD_SIZE, HEAD_SIZE)
    eye = jnp.eye(HEADS_PER_GROUP, dtype=s.dtype)
    out = jnp.einsum("bgivk,ij->bgivjk", s, eye)
    return out.reshape(bsz, HEAD_GROUPS, MXU_WIDTH, MXU_WIDTH)


def _blockdiag_to_state(s_bd):
    bsz = s_bd.shape[0]
    s = s_bd.reshape(bsz, HEAD_GROUPS, HEADS_PER_GROUP, HEAD_SIZE, HEADS_PER_GROUP, HEAD_SIZE)
    idx = jnp.arange(HEADS_PER_GROUP)
    s = s[:, :, idx, :, idx, :]
    s = jnp.moveaxis(s, 0, 2)
    return s.reshape(bsz, RWKV_HEADS, HEAD_SIZE, HEAD_SIZE)


def _conv_kernel(tm, glu_ref, past_ref, w_ref, b_ref, lnw_ref, lnb_ref, out_ref, new_ref, buf):
    hist = CONV_WIDTH - 1
    pad = CONV_HALO - hist

    @pl.when(pl.program_id(1) == 0)
    def _():
        buf[0:pad, :] = jnp.zeros((pad, CONV_CH), F32)
        buf[pad:CONV_HALO, :] = past_ref[0]

    buf[CONV_HALO:CONV_HALO + tm, :] = glu_ref[0]
    blk = min(CONV_ROW_BLOCK, tm)
    for rb in range(tm // blk):
        base = rb * blk + pad
        acc = jnp.zeros((blk, CONV_CH), F32) + b_ref[...]
        for j in range(CONV_WIDTH):
            acc = acc + w_ref[j:j + 1, :] * buf[base + j:base + j + blk, :]
        mean = jnp.mean(acc, axis=-1, keepdims=True)
        cen = acc - mean
        var = jnp.mean(cen * cen, axis=-1, keepdims=True)
        yn = cen * lax.rsqrt(var + LN_EPS) * lnw_ref[...] + lnb_ref[...]
        out_ref[0, rb * blk:(rb + 1) * blk, :] = _silu(yn)
    new_ref[0] = buf[tm + pad:tm + CONV_HALO, :]
    buf[0:CONV_HALO, :] = buf[tm:tm + CONV_HALO, :]


def _conv_call(glu, past, p, tm):
    bsz, seq, ch = glu.shape
    hist = CONV_WIDTH - 1
    tile = pl.BlockSpec((1, tm, ch), lambda b, t: (b, t, 0))
    st = pl.BlockSpec((1, hist, ch), lambda b, t: (b, 0, 0))
    vec = _const_spec((1, ch))
    return pl.pallas_call(
        functools.partial(_conv_kernel, tm),
        grid=(bsz, seq // tm),
        in_specs=[tile, st, _const_spec((CONV_WIDTH, ch)), vec, vec, vec],
        out_specs=[tile, st],
        out_shape=[jax.ShapeDtypeStruct((bsz, seq, ch), F32),
                   jax.ShapeDtypeStruct((bsz, hist, ch), F32)],
        scratch_shapes=[pltpu.VMEM((CONV_HALO + tm, ch), F32)],
        compiler_params=pltpu.CompilerParams(
            dimension_semantics=("arbitrary", "arbitrary"), vmem_limit_bytes=VMEM_LIMIT),
        name="conformer_conv",
    )(glu, past, p["conv_w"], p["conv_b"], p["cln_w"], p["cln_b"])


def _ffn_kernel(final, tm, x_ref, a_ref, b_ref, ada_ref, woa_ref, wob_ref, g2_ref, up_ref, cw_ref,
                cb_ref, down_ref, past_ref, fg_ref, out_ref, new_ref, h2_sc, acc_sc, carry):
    @pl.when(pl.program_id(1) == 0)
    def _():
        carry[...] = past_ref[0]

    ada = ada_ref[0]
    mix = _dot(a_ref[0].astype(BF16), woa_ref[...]) + _dot(b_ref[0].astype(BF16), wob_ref[...])
    x1 = x_ref[0] + ada[2:3] * mix
    h2 = _rms_norm(x1, g2_ref[...]) * (1.0 + ada[4:5]) + ada[3:4]
    h2_sc[...] = h2.astype(BF16)
    acc_sc[...] = jnp.zeros((tm, D_MODEL), F32)

    def conv_half(j):
        u = _dot(h2_sc[...], up_ref[j])
        past = carry[j]
        cw = cw_ref[j]
        y = (cw[0:1] * _shift_rows(u, past, 2) + cw[1:2] * _shift_rows(u, past, 1)
             + cw[2:3] * u + cb_ref[j])
        carry[j] = u[tm - (FFN_CONV_WIDTH - 1):tm]
        return y

    def slab(j, c):
        gate = conv_half(j)
        val = conv_half(j + FFN_SLABS)
        act = (_silu(gate) * val).astype(BF16)
        acc_sc[...] += _dot(act, down_ref[j])
        return c

    lax.fori_loop(0, FFN_SLABS, slab, 0)
    x2 = x1 + ada[5:6] * acc_sc[...]
    out_ref[0] = _rms_norm(x2, fg_ref[...]) if final else x2
    new_ref[0] = carry[...]


def _ffn_call(x, a_out, b_out, ada, past, p, final_g, final, tm):
    bsz, seq, _ = x.shape
    hist = FFN_CONV_WIDTH - 1
    tile = lambda width: pl.BlockSpec((1, tm, width), lambda b, t: (b, t, 0))
    st = pl.BlockSpec((1, 2 * FFN_SLABS, hist, FFN_SLAB), lambda b, t: (b, 0, 0, 0))
    half = RWKV_WIDTH
    return pl.pallas_call(
        functools.partial(_ffn_kernel, final, tm),
        grid=(bsz, seq // tm),
        in_specs=[tile(D_MODEL), tile(half), tile(CONV_CH),
                  pl.BlockSpec((1, 6, D_MODEL), lambda b, t: (b, 0, 0)),
                  _const_spec((half, D_MODEL)), _const_spec((CONV_CH, D_MODEL)),
                  _const_spec((1, D_MODEL)),
                  _const_spec((2 * FFN_SLABS, D_MODEL, FFN_SLAB)),
                  _const_spec((2 * FFN_SLABS, FFN_CONV_WIDTH, FFN_SLAB)),
                  _const_spec((2 * FFN_SLABS, 1, FFN_SLAB)),
                  _const_spec((FFN_SLABS, FFN_SLAB, D_MODEL)),
                  st, _const_spec((1, D_MODEL))],
        out_specs=[tile(D_MODEL), st],
        out_shape=[jax.ShapeDtypeStruct((bsz, seq, D_MODEL), F32),
                   jax.ShapeDtypeStruct((bsz, 2 * FFN_SLABS, hist, FFN_SLAB), F32)],
        scratch_shapes=[pltpu.VMEM((tm, D_MODEL), BF16), pltpu.VMEM((tm, D_MODEL), F32),
                        pltpu.VMEM((2 * FFN_SLABS, hist, FFN_SLAB), F32)],
        compiler_params=pltpu.CompilerParams(
            dimension_semantics=("arbitrary", "arbitrary"), vmem_limit_bytes=VMEM_LIMIT),
        name="mix_out_ffn",
    )(x, a_out, b_out, ada, p["w_out_a"], p["w_out_b"], p["norm2_g"], p["ffn_up"], p["ffn_conv_w"],
      p["ffn_conv_b"], p["ffn_down"], past, final_g)


def _layer_params(l, norm1_g, w_in, mu_shift, w0, w2, a0, a2, g2, k_k, k_a, r_k, lnx_w, lnx_b, conv_w,
                  conv_b, cln_w, cln_b, w_out, norm2_g, ffn_up, ffn_conv_w, ffn_conv_b, ffn_down):
    row = lambda t: t[l].reshape(1, -1)
    zeros = jnp.zeros((DECAY_LORA, RWKV_WIDTH), F32)
    lora = jnp.concatenate([jnp.concatenate([w2[l], zeros], axis=1),
                            jnp.concatenate([zeros, a2[l]], axis=1)], axis=0)
    slabs = 2 * FFN_SLABS
    return dict(
        norm1_g=row(norm1_g), w_in=w_in[l].astype(BF16), mu_shift=row(mu_shift), w0=row(w0),
        lora=lora.astype(BF16), a0=row(a0), g2=g2[l].astype(BF16), k_k=row(k_k), k_a=row(k_a),
        r_k=row(r_k), lnx_w=row(lnx_w), lnx_b=row(lnx_b), conv_w=conv_w[l], conv_b=row(conv_b),
        cln_w=row(cln_w), cln_b=row(cln_b),
        w_out_a=w_out[l, :RWKV_WIDTH].astype(BF16), w_out_b=w_out[l, RWKV_WIDTH:].astype(BF16),
        norm2_g=row(norm2_g),
        ffn_up=ffn_up[l].astype(BF16).reshape(D_MODEL, slabs, FFN_SLAB).transpose(1, 0, 2),
        ffn_conv_w=ffn_conv_w[l].reshape(FFN_CONV_WIDTH, slabs, FFN_SLAB).transpose(1, 0, 2),
        ffn_conv_b=ffn_conv_b[l].reshape(slabs, 1, FFN_SLAB),
        ffn_down=ffn_down[l].astype(BF16).reshape(FFN_SLABS, FFN_SLAB, D_MODEL),
    )


def _trunk(x, ada, st_wkv, st_shift, st_conv, st_ffn, params, vres, final_g):
    bsz, seq, _ = x.shape
    tm = min(MAX_TIME_TILE, seq)
    chunk = min(MAX_CHUNK, seq)
    hist = FFN_CONV_WIDTH - 1
    slabs = 2 * FFN_SLABS
    v_first = None
    new = ([], [], [], [])
    for l in range(DEPTH):
        p = params[l]
        r, lw, k, v, kk, a, g, glu, n_shift = _mix_in_call(
            x, ada[l], st_shift[l], p, None if l == 0 else vres[l - 1], v_first, tm)
        if l == 0:
            v_first = v
        a_out, s_bd = _wkv_call(r, lw, k, v, kk, a, g, _state_to_blockdiag(st_wkv[l]), p, tm, chunk)
        b_out, n_conv = _conv_call(glu, st_conv[l], p, tm)
        past = st_ffn[l].reshape(bsz, hist, slabs, FFN_SLAB).transpose(0, 2, 1, 3)
        x, n_ffn = _ffn_call(x, a_out, b_out, ada[l], past, p, final_g, l == DEPTH - 1, tm)
        new[0].append(_blockdiag_to_state(s_bd))
        new[1].append(n_shift)
        new[2].append(n_conv)
        new[3].append(n_ffn.transpose(0, 2, 1, 3).reshape(bsz, hist, 2 * D_FF))
    return x, [jnp.stack(t) for t in new]


def kernel(x_prompt, x_sample, c_prompt, c_sample, state_wkv, state_shift, state_conv, state_ffn_conv,
           norm1_g, ada_w, ada_b, w_in, mu_shift, w0, w2, a0, a2, g2, k_k, k_a, r_k, v0, v1, v2,
           lnx_w, lnx_b, conv_w, conv_b, cln_w, cln_b, w_out, norm2_g, ffn_up, ffn_conv_w, ffn_conv_b,
           ffn_down, final_g):
    bp = x_prompt.shape[0]
    bs = x_sample.shape[0]
    params = [_layer_params(l, norm1_g, w_in, mu_shift, w0, w2, a0, a2, g2, k_k, k_a, r_k, lnx_w, lnx_b,
                            conv_w, conv_b, cln_w, cln_b, w_out, norm2_g, ffn_up, ffn_conv_w,
                            ffn_conv_b, ffn_down) for l in range(DEPTH)]
    vres = [(v0[l].reshape(1, -1), v1[l].astype(BF16), v2[l].astype(BF16)) for l in range(DEPTH - 1)]
    fg = final_g.reshape(1, -1)

    c_all = jnp.concatenate([c_prompt, c_sample], axis=0)
    pad = (-c_all.shape[0]) % 8
    c_all = jnp.concatenate([c_all, jnp.zeros((pad, D_MODEL), F32)], axis=0)
    ada = _ada_call(c_all, ada_w, ada_b).reshape(DEPTH, -1, 6, D_MODEL)
    ada_p, ada_s = ada[:, :bp], ada[:, bp:bp + bs]

    z_wkv = jnp.zeros((DEPTH, bp, RWKV_HEADS, HEAD_SIZE, HEAD_SIZE), F32)
    z_shift = jnp.zeros((DEPTH, bp, 1, RWKV_COLS), F32)
    z_conv = jnp.zeros((DEPTH, bp, CONV_WIDTH - 1, CONV_CH), F32)
    z_ffn = jnp.zeros((DEPTH, bp, FFN_CONV_WIDTH - 1, 2 * D_FF), F32)
    y_p, (wkv_p, shift_p, conv_p, ffn_p) = _trunk(
        x_prompt, ada_p, z_wkv, z_shift, z_conv, z_ffn, params, vres, fg)
    y_s, (wkv_s, shift_s, conv_s, ffn_s) = _trunk(
        x_sample, ada_s, state_wkv, state_shift, state_conv, state_ffn_conv, params, vres, fg)
    return (y_p, y_s, wkv_p, shift_p, conv_p, ffn_p, wkv_s, shift_s, conv_s, ffn_s)
```

```python
import functools

import jax
import jax.numpy as jnp
from jax import lax
from jax.experimental import pallas as pl
from jax.experimental.pallas import tpu as pltpu

D_MODEL = 1024
DEPTH = 2
RWKV_WIDTH = 512
HEAD_SIZE = 64
RWKV_HEADS = RWKV_WIDTH // HEAD_SIZE
DECAY_LORA = 64
ICLR_LORA = 64
GATE_LORA = 128
RWKV_COLS = 3 * RWKV_WIDTH + DECAY_LORA + ICLR_LORA + GATE_LORA
CONV_CH = 512
CONV_WIDTH = 31
IN_COLS = RWKV_COLS + 2 * CONV_CH
D_FF = 2816
FFN_CONV_WIDTH = 3
NORM_EPS = 1e-6
LN_EPS = 1e-5
GN_EPS = 64e-5

MXU_WIDTH = 256
HEADS_PER_GROUP = MXU_WIDTH // HEAD_SIZE
HEAD_GROUPS = RWKV_HEADS // HEADS_PER_GROUP
FFN_SLAB = MXU_WIDTH
FFN_SLABS = D_FF // FFN_SLAB
MAX_CHUNK = 64
MAX_TIME_TILE = 512
CONV_HALO = 32
CONV_ROW_BLOCK = 64
VMEM_LIMIT = 56 * 1024 * 1024
INTRA_PASSES = 1
STATE_PASSES = 1
WKV_LOCKSTEP_CHUNKS = 4
INTRA_PASSES_PARTS = 1 if INTRA_PASSES == 1 else 2
STATE_PASSES_PARTS = 1 if STATE_PASSES == 1 else 2

F32 = jnp.float32
BF16 = jnp.bfloat16


def _dot(a, b):
    return jnp.dot(a, b, preferred_element_type=F32)


def _dot_nt(a, b):
    return lax.dot_general(a, b, (((1,), (1,)), ((), ())), preferred_element_type=F32)


def _split2(x):
    hi = x.astype(BF16)
    lo = (x - hi.astype(F32)).astype(BF16)
    return hi, lo


def _split3(x):
    hi = x.astype(BF16)
    rem = x - hi.astype(F32)
    mid = rem.astype(BF16)
    lo = (rem - mid.astype(F32)).astype(BF16)
    return hi, mid, lo


def _mm3(a, b_parts, dot=_dot):
    a_hi, a_lo = _split2(a)
    b_hi, b_lo = b_parts
    return dot(a_hi, b_hi) + dot(a_hi, b_lo) + dot(a_lo, b_hi)


def _head_sum(x, seg):
    hi, lo = _split2(x)
    return _dot(hi, seg) + _dot(lo, seg)


def _log2(n):
    assert n & (n - 1) == 0
    return n.bit_length() - 1


def _div(x, n):
    return x >> _log2(n)


def _mod(x, n):
    return x & (n - 1)


def _head_seg_matrix(width):
    row = _div(lax.broadcasted_iota(jnp.int32, (width, width), 0), HEAD_SIZE)
    col = _div(lax.broadcasted_iota(jnp.int32, (width, width), 1), HEAD_SIZE)
    return (row == col).astype(BF16)


def _sigmoid(x):
    return 1.0 / (1.0 + jnp.exp(-x))


def _silu(x):
    return x * _sigmoid(x)


def _rms_norm(x, g):
    return x * lax.rsqrt(jnp.mean(x * x, axis=-1, keepdims=True) + NORM_EPS) * g


def _shift_rows(x, carry_rows, shift):
    rows = lax.broadcasted_iota(jnp.int32, x.shape, 0)
    out = pltpu.roll(x, shift, 0)
    n_carry = carry_rows.shape[0]
    for i in range(shift):
        out = jnp.where(rows == i, carry_rows[n_carry - shift + i:n_carry - shift + i + 1], out)
    return out


def _ada_kernel(c_ref, w_ref, b_ref, o_ref):
    c = c_ref[...]
    o_ref[0] = _dot(_silu(c).astype(BF16), w_ref[0].astype(BF16)) + b_ref[0]


def _ada_call(c_all, ada_w, ada_b):
    rows = c_all.shape[0]
    n_tile = 1536
    return pl.pallas_call(
        _ada_kernel,
        grid=(DEPTH, 6 * D_MODEL // n_tile),
        in_specs=[
            pl.BlockSpec((rows, D_MODEL), lambda l, n: (0, 0)),
            pl.BlockSpec((1, D_MODEL, n_tile), lambda l, n: (l, 0, n)),
            pl.BlockSpec((1, 1, n_tile), lambda l, n: (l, 0, n)),
        ],
        out_specs=pl.BlockSpec((1, rows, n_tile), lambda l, n: (l, 0, n)),
        out_shape=jax.ShapeDtypeStruct((DEPTH, rows, 6 * D_MODEL), F32),
        compiler_params=pltpu.CompilerParams(
            dimension_semantics=("arbitrary", "arbitrary"), vmem_limit_bytes=VMEM_LIMIT),
        name="ada_vectors",
    )(c_all, ada_w, ada_b.reshape(DEPTH, 1, 6 * D_MODEL))


def _mix_in_kernel(has_vres, tm, *refs):
    (x_ref, ada_ref, g1_ref, win_ref, sh0_ref, mu_ref, w0_ref, lora_ref, a0_ref, g2_ref,
     kk_ref, ka_ref) = refs[:12]
    refs = refs[12:]
    if has_vres:
        v0_ref, v1_ref, v2_ref, vf_ref = refs[:4]
        refs = refs[4:]
    r_o, lw_o, k_o, v_o, kkn_o, a_o, g_o, glu_o, shift_o, carry = refs

    @pl.when(pl.program_id(1) == 0)
    def _():
        carry[...] = sh0_ref[0]

    x = x_ref[0]
    ada = ada_ref[0]
    h = _rms_norm(x, g1_ref[...]) * (1.0 + ada[1:2]) + ada[0:1]
    proj = _dot(h.astype(BF16), win_ref[...])
    rw = proj[:, :RWKV_COLS]
    cv = proj[:, RWKV_COLS:]
    glu_o[0] = cv[:, :CONV_CH] * _sigmoid(cv[:, CONV_CH:])

    last = rw[tm - 1:tm]
    shift_o[0] = last
    prev = _shift_rows(rw, carry[...], 1)
    carry[...] = last
    xs = rw + (prev - rw) * mu_ref[...]

    w = RWKV_WIDTH
    xr, xk, xv = xs[:, :w], xs[:, w:2 * w], xs[:, 2 * w:3 * w]
    x_lora = xs[:, 3 * w:3 * w + DECAY_LORA + ICLR_LORA]
    xg = xs[:, 3 * w + DECAY_LORA + ICLR_LORA:]
    lane = lax.broadcasted_iota(jnp.int32, x_lora.shape, 1)
    lora_in = jnp.where(lane < DECAY_LORA, jnp.tanh(x_lora), x_lora)
    lora = _dot(lora_in.astype(BF16), lora_ref[...])

    z = -(w0_ref[...] + lora[:, :w])
    softplus = jnp.maximum(z, 0.0) + jnp.log(1.0 + jnp.exp(-jnp.abs(z)))
    lw_o[0] = -jnp.exp(-softplus - 0.5)
    a = _sigmoid(a0_ref[...] + lora[:, w:])
    a_o[0] = a
    g_o[0] = _dot(_sigmoid(xg).astype(BF16), g2_ref[...])

    if has_vres:
        gate = _dot(_dot(xv.astype(BF16), v1_ref[...]).astype(BF16), v2_ref[...])
        v_o[0] = xv + (vf_ref[0] - xv) * _sigmoid(v0_ref[...] + gate)
    else:
        v_o[0] = xv

    kk = xk * kk_ref[...]
    norm = jnp.sqrt(_head_sum(kk * kk, _head_seg_matrix(w)))
    kkn_o[0] = kk / jnp.maximum(norm, 1e-12)
    k_o[0] = xk * (1.0 + (a - 1.0) * ka_ref[...])
    r_o[0] = xr


def _const_spec(shape):
    return pl.BlockSpec(shape, lambda b, t: (0,) * len(shape), pipeline_mode=pl.Buffered(1))


def _mix_in_call(x, ada, shift0, p, vres, v_first, tm):
    bsz, seq, _ = x.shape
    w = RWKV_WIDTH
    tile = lambda width: pl.BlockSpec((1, tm, width), lambda b, t: (b, t, 0))
    per_batch = lambda rows, width: pl.BlockSpec((1, rows, width), lambda b, t: (b, 0, 0))
    args = [x, ada, p["norm1_g"], p["w_in"], shift0, p["mu_shift"], p["w0"], p["lora"], p["a0"],
            p["g2"], p["k_k"], p["k_a"]]
    in_specs = [tile(D_MODEL), per_batch(6, D_MODEL), _const_spec((1, D_MODEL)),
                _const_spec((D_MODEL, IN_COLS)), per_batch(1, RWKV_COLS), _const_spec((1, RWKV_COLS)),
                _const_spec((1, w)), _const_spec((DECAY_LORA + ICLR_LORA, 2 * w)), _const_spec((1, w)),
                _const_spec((GATE_LORA, w)), _const_spec((1, w)), _const_spec((1, w))]
    if vres is not None:
        v0, v1, v2 = vres
        args += [v0, v1, v2, v_first]
        in_specs += [_const_spec((1, w)), _const_spec(v1.shape), _const_spec(v2.shape), tile(w)]
    act = jax.ShapeDtypeStruct((bsz, seq, w), F32)
    return pl.pallas_call(
        functools.partial(_mix_in_kernel, vres is not None, tm),
        grid=(bsz, seq // tm),
        in_specs=in_specs,
        out_specs=[tile(w)] * 8 + [per_batch(1, RWKV_COLS)],
        out_shape=[act] * 8 + [jax.ShapeDtypeStruct((bsz, 1, RWKV_COLS), F32)],
        scratch_shapes=[pltpu.VMEM((1, RWKV_COLS), F32)],
        compiler_params=pltpu.CompilerParams(
            dimension_semantics=("arbitrary", "arbitrary"), vmem_limit_bytes=VMEM_LIMIT),
        name="mix_in",
    )(*args)


def _parts(x, passes):
    return (x.astype(BF16),) if passes == 1 else _split2(x)


def _mm(a_parts, b_parts, dot=_dot):
    out = dot(a_parts[0], b_parts[0])
    if len(a_parts) > 1:
        out = out + dot(a_parts[1], b_parts[0]) + dot(a_parts[0], b_parts[1])
    return out


def _wkv_kernel(tm, chunk, r_ref, lw_ref, k_ref, v_ref, kk_ref, a_ref, g_ref, s0_ref, rk_ref,
                lnw_ref, lnb_ref, out_ref, sn_ref, state, lhs_sc, q_sc, rkv_sc, arb_sc, bdec_sc,
                vk_sc, gl_sc):
    n_heads = HEADS_PER_GROUP
    gw = MXU_WIDTH
    side = n_heads * chunk
    n_chunks = tm // chunk
    lockstep = min(WKV_LOCKSTEP_CHUNKS, n_chunks)

    @pl.when(pl.program_id(1) == 0)
    def _():
        state[...] = s0_ref[0]

    def iota(shape, dim):
        return lax.broadcasted_iota(jnp.int32, shape, dim)

    mask_data = _div(iota((side, gw), 0), chunk) == _div(iota((side, gw), 1), HEAD_SIZE)
    mask_side = _div(iota((side, side), 0), chunk) == _div(iota((side, side), 1), chunk)
    mask_state = _div(iota((gw, gw), 0), HEAD_SIZE) == _div(iota((gw, gw), 1), HEAD_SIZE)
    t_idx = iota((chunk, side), 0)
    s_idx = _mod(iota((chunk, side), 1), chunk)
    strict = s_idx < t_idx
    incl = s_idx <= t_idx
    eye = (s_idx == t_idx).astype(F32)
    row_idx = iota((chunk, RWKV_WIDTH), 0)

    def expand(parts, mask):
        zero = jnp.zeros((), BF16)
        return tuple(jnp.where(mask, jnp.concatenate([q] * n_heads, axis=0), zero) for q in parts)

    def expand_data(x, passes=INTRA_PASSES):
        return expand(_parts(x, passes), mask_data)

    def expand_side(x):
        return expand(_parts(x, INTRA_PASSES), mask_side)

    def intra(x):
        return _parts(x, INTRA_PASSES)

    def on_state(x):
        return _parts(x, STATE_PASSES)

    def lower_left(blk):
        return ((_div(t_idx, 2 * blk) == _div(s_idx, 2 * blk))
                & (_mod(_div(t_idx, blk), 2) == 1) & (_mod(_div(s_idx, blk), 2) == 0))

    def precompute(step_idx, carry):
        chains = []
        for j in range(lockstep):
            c = step_idx * lockstep + j
            rows = pl.ds(pl.multiple_of(c * chunk, chunk), chunk)
            r = r_ref[0, rows, :]
            lw = lw_ref[0, rows, :]
            k = k_ref[0, rows, :]
            v = v_ref[0, rows, :]
            kk = kk_ref[0, rows, :]
            a = a_ref[0, rows, :]

            cum = lw
            step = 1
            while step < chunk:
                cum = cum + jnp.where(row_idx >= step, pltpu.roll(cum, step, 0), 0.0)
                step *= 2
            total = cum[chunk - 1:chunk]
            decay_inv = jnp.exp(-cum)
            decay_out = jnp.exp(total - cum)
            gl_sc[c] = jnp.exp(total)

            bhat = kk * a
            qa = jnp.concatenate([-kk * jnp.exp(cum - lw), r * jnp.exp(cum)], axis=0)
            kb_b = bhat * decay_inv
            kb_k = k * decay_inv
            bdec = bhat * decay_out
            kdec = k * decay_out
            for grp in range(HEAD_GROUPS):
                lanes = slice(grp * gw, (grp + 1) * gw)
                chains.append(dict(c=c, grp=grp, qa=qa[:, lanes], v=v[:, lanes], kb_b=kb_b[:, lanes],
                                   kb_k=kb_k[:, lanes], bdec=bdec[:, lanes], kdec=kdec[:, lanes]))

        for ch in chains:
            qa_p = intra(ch["qa"])
            a_b = _mm(qa_p, expand_data(ch["kb_b"]), _dot_nt)
            a_k = _mm(qa_p, expand_data(ch["kb_k"]), _dot_nt)
            ch["a_ab"] = jnp.where(strict, a_b[:chunk], 0.0)
            ch["a_rb"] = jnp.where(incl, a_b[chunk:], 0.0)
            ch["a_k"] = jnp.where(jnp.concatenate([strict, incl], axis=0), a_k, 0.0)
            ch["tinv"] = eye + jnp.where(lower_left(1), ch["a_ab"], 0.0)
        blk = 2
        while blk < chunk:
            for ch in chains:
                ch["inner"] = _mm(intra(jnp.where(lower_left(blk), ch["a_ab"], 0.0)),
                                  expand_side(ch["tinv"]))
            for ch in chains:
                ch["tinv"] = ch["tinv"] + _mm(intra(ch["tinv"]), expand_side(ch["inner"]))
            blk *= 2

        for ch in chains:
            ch["tinv_p"] = intra(ch["tinv"])
            ch["kv"] = _mm(intra(ch["a_k"]), expand_data(ch["v"]))
            ch["pmat"] = _mm(ch["tinv_p"], expand_data(ch["qa"][:chunk]))
            vk = _mm(on_state(ch["v"].T), on_state(ch["kdec"]))
            vk_sc[ch["c"], ch["grp"]] = jnp.where(mask_state, vk, 0.0)
        for ch in chains:
            c, grp = ch["c"], ch["grp"]
            q_sc[c, grp] = _mm(ch["tinv_p"], expand_data(ch["kv"][:chunk]))
            rkv_sc[c, grp] = ch["kv"][chunk:]
            lhs = jnp.concatenate([ch["pmat"], ch["qa"][chunk:]], axis=0)
            for i, part in enumerate(on_state(lhs)):
                lhs_sc[c, grp, i] = part
            for i, part in enumerate(on_state(ch["bdec"])):
                bdec_sc[c, grp, i] = part
            for i, part in enumerate(intra(ch["a_rb"])):
                arb_sc[c, grp, i] = part
        return carry

    def advance(c, carry):
        rows = pl.ds(pl.multiple_of(c * chunk, chunk), chunk)
        decay_all = gl_sc[c]
        for grp in range(HEAD_GROUPS):
            lanes = slice(grp * gw, (grp + 1) * gw)
            s_g = state[grp]
            lhs = tuple(lhs_sc[c, grp, i] for i in range(STATE_PASSES_PARTS))
            xu = _mm(lhs, on_state(s_g), _dot_nt)
            u = xu[:chunk] + q_sc[c, grp]
            arb = tuple(arb_sc[c, grp, i] for i in range(INTRA_PASSES_PARTS))
            out_ref[0, rows, lanes] = xu[chunk:] + rkv_sc[c, grp] + _mm(arb, expand_data(u))
            bd = tuple(bdec_sc[c, grp, i] for i in range(STATE_PASSES_PARTS))
            upd = _mm(on_state(u.T), bd)
            state[grp] = s_g * decay_all[:, lanes] + vk_sc[c, grp] + jnp.where(mask_state, upd, 0.0)
        return carry

    lax.fori_loop(0, n_chunks // lockstep, precompute, 0)
    lax.fori_loop(0, n_chunks, advance, 0)
    sn_ref[0] = state[...]

    seg = _head_seg_matrix(gw)

    def head_sum(x):
        hi, lo = _split2(jnp.concatenate([x[:, :gw], x[:, gw:]], axis=0))
        s = _dot(jnp.concatenate([hi, lo], axis=0), seg)
        s = s[:2 * tm] + s[2 * tm:]
        return jnp.concatenate([s[:tm], s[tm:]], axis=1)

    y = out_ref[0]
    inv_n = 1.0 / HEAD_SIZE
    cen = y - head_sum(y) * inv_n
    var = head_sum(cen * cen) * inv_n
    yn = cen * lax.rsqrt(var + GN_EPS) * lnw_ref[...] + lnb_ref[...]
    bonus = head_sum(r_ref[0] * k_ref[0] * rk_ref[...]) * v_ref[0]
    out_ref[0] = (yn + bonus) * g_ref[0]


def _wkv_call(r, lw, k, v, kk, a, g, s0_bd, p, tm, chunk):
    bsz, seq, w = r.shape
    tile = pl.BlockSpec((1, tm, w), lambda b, t: (b, t, 0))
    st = pl.BlockSpec((1, HEAD_GROUPS, MXU_WIDTH, MXU_WIDTH), lambda b, t: (b, 0, 0, 0))
    vec = _const_spec((1, w))
    gw = MXU_WIDTH
    side = HEADS_PER_GROUP * chunk
    n_chunks = tm // chunk
    return pl.pallas_call(
        functools.partial(_wkv_kernel, tm, chunk),
        grid=(bsz, seq // tm),
        in_specs=[tile] * 7 + [st, vec, vec, vec],
        out_specs=[tile, st],
        out_shape=[jax.ShapeDtypeStruct((bsz, seq, w), F32),
                   jax.ShapeDtypeStruct((bsz, HEAD_GROUPS, MXU_WIDTH, MXU_WIDTH), F32)],
        scratch_shapes=[
            pltpu.VMEM((HEAD_GROUPS, gw, gw), F32),
            pltpu.VMEM((n_chunks, HEAD_GROUPS, STATE_PASSES_PARTS, 2 * chunk, gw), BF16),
            pltpu.VMEM((n_chunks, HEAD_GROUPS, chunk, gw), F32),
            pltpu.VMEM((n_chunks, HEAD_GROUPS, chunk, gw), F32),
            pltpu.VMEM((n_chunks, HEAD_GROUPS, INTRA_PASSES_PARTS, chunk, side), BF16),
            pltpu.VMEM((n_chunks, HEAD_GROUPS, STATE_PASSES_PARTS, chunk, gw), BF16),
            pltpu.VMEM((n_chunks, HEAD_GROUPS, gw, gw), F32),
            pltpu.VMEM((n_chunks, 1, w), F32),
        ],
        compiler_params=pltpu.CompilerParams(
            dimension_semantics=("arbitrary", "arbitrary"), vmem_limit_bytes=VMEM_LIMIT),
        name="wkv7_chunked",
    )(r, lw, k, v, kk, a, g, s0_bd, p["r_k"], p["lnx_w"], p["lnx_b"])


def _state_to_blockdiag(s):
    bsz = s.shape[0]
    s = s.reshape(bsz, HEAD_GROUPS, HEADS_PER_GROUP, HEAD_SIZE, HEAD_SIZE)
    eye = jnp.eye(HEADS_PER_GROUP, dtype=s.dtype)
    out = jnp.einsum("bgivk,ij->bgivjk", s, eye)
    return out.reshape(bsz, HEAD_GROUPS, MXU_WIDTH, MXU_WIDTH)


def _blockdiag_to_state(s_bd):
    bsz = s_bd.shape[0]
    s = s_bd.reshape(bsz, HEAD_GROUPS, HEADS_PER_GROUP, HEAD_SIZE, HEADS_PER_GROUP, HEAD_SIZE)
    idx = jnp.arange(HEADS_PER_GROUP)
    s = s[:, :, idx, :, idx, :]
    s = jnp.moveaxis(s, 0, 2)
    return s.reshape(bsz, RWKV_HEADS, HEAD_SIZE, HEAD_SIZE)


def _conv_kernel(tm, glu_ref, past_ref, w_ref, b_ref, lnw_ref, lnb_ref, out_ref, new_ref, buf):
    hist = CONV_WIDTH - 1
    pad = CONV_HALO - hist

    @pl.when(pl.program_id(1) == 0)
    def _():
        buf[0:pad, :] = jnp.zeros((pad, CONV_CH), F32)
        buf[pad:CONV_HALO, :] = past_ref[0]

    buf[CONV_HALO:CONV_HALO + tm, :] = glu_ref[0]
    blk = min(CONV_ROW_BLOCK, tm)
    for rb in range(tm // blk):
        base = rb * blk + pad
        acc = jnp.zeros((blk, CONV_CH), F32) + b_ref[...]
        for j in range(CONV_WIDTH):
            acc = acc + w_ref[j:j + 1, :] * buf[base + j:base + j + blk, :]
        mean = jnp.mean(acc, axis=-1, keepdims=True)
        cen = acc - mean
        var = jnp.mean(cen * cen, axis=-1, keepdims=True)
        yn = cen * lax.rsqrt(var + LN_EPS) * lnw_ref[...] + lnb_ref[...]
        out_ref[0, rb * blk:(rb + 1) * blk, :] = _silu(yn)
    new_ref[0] = buf[tm + pad:tm + CONV_HALO, :]
    buf[0:CONV_HALO, :] = buf[tm:tm + CONV_HALO, :]


def _conv_call(glu, past, p, tm):
    bsz, seq, ch = glu.shape
    hist = CONV_WIDTH - 1
    tile = pl.BlockSpec((1, tm, ch), lambda b, t: (b, t, 0))
    st = pl.BlockSpec((1, hist, ch), lambda b, t: (b, 0, 0))
    vec = _const_spec((1, ch))
    return pl.pallas_call(
        functools.partial(_conv_kernel, tm),
        grid=(bsz, seq // tm),
        in_specs=[tile, st, _const_spec((CONV_WIDTH, ch)), vec, vec, vec],
        out_specs=[tile, st],
        out_shape=[jax.ShapeDtypeStruct((bsz, seq, ch), F32),
                   jax.ShapeDtypeStruct((bsz, hist, ch), F32)],
        scratch_shapes=[pltpu.VMEM((CONV_HALO + tm, ch), F32)],
        compiler_params=pltpu.CompilerParams(
            dimension_semantics=("arbitrary", "arbitrary"), vmem_limit_bytes=VMEM_LIMIT),
        name="conformer_conv",
    )(glu, past, p["conv_w"], p["conv_b"], p["cln_w"], p["cln_b"])


def _ffn_kernel(final, tm, x_ref, a_ref, b_ref, ada_ref, woa_ref, wob_ref, g2_ref, up_ref, cw_ref,
                cb_ref, down_ref, past_ref, fg_ref, out_ref, new_ref, h2_sc, acc_sc, carry):
    @pl.when(pl.program_id(1) == 0)
    def _():
        carry[...] = past_ref[0]

    ada = ada_ref[0]
    mix = _dot(a_ref[0].astype(BF16), woa_ref[...]) + _dot(b_ref[0].astype(BF16), wob_ref[...])
    x1 = x_ref[0] + ada[2:3] * mix
    h2 = _rms_norm(x1, g2_ref[...]) * (1.0 + ada[4:5]) + ada[3:4]
    h2_sc[...] = h2.astype(BF16)
    acc_sc[...] = jnp.zeros((tm, D_MODEL), F32)

    def up_proj(j):
        return _dot(h2_sc[...], up_ref[j]), _dot(h2_sc[...], up_ref[j + FFN_SLABS])

    def causal_conv(u, j):
        past = carry[j]
        cw = cw_ref[j]
        y = (cw[0:1] * _shift_rows(u, past, 2) + cw[1:2] * _shift_rows(u, past, 1)
             + cw[2:3] * u + cb_ref[j])
        carry[j] = u[tm - (FFN_CONV_WIDTH - 1):tm]
        return y

    ahead = up_proj(0)
    for j in range(FFN_SLABS):
        u_gate, u_val = ahead
        if j + 1 < FFN_SLABS:
            ahead = up_proj(j + 1)
        act = _silu(causal_conv(u_gate, j)) * causal_conv(u_val, j + FFN_SLABS)
        acc_sc[...] += _dot(act.astype(BF16), down_ref[j])
    x2 = x1 + ada[5:6] * acc_sc[...]
    out_ref[0] = _rms_norm(x2, fg_ref[...]) if final else x2
    new_ref[0] = carry[...]


def _ffn_call(x, a_out, b_out, ada, past, p, final_g, final, tm):
    bsz, seq, _ = x.shape
    hist = FFN_CONV_WIDTH - 1
    tile = lambda width: pl.BlockSpec((1, tm, width), lambda b, t: (b, t, 0))
    st = pl.BlockSpec((1, 2 * FFN_SLABS, hist, FFN_SLAB), lambda b, t: (b, 0, 0, 0))
    half = RWKV_WIDTH
    return pl.pallas_call(
        functools.partial(_ffn_kernel, final, tm),
        grid=(bsz, seq // tm),
        in_specs=[tile(D_MODEL), tile(half), tile(CONV_CH),
                  pl.BlockSpec((1, 6, D_MODEL), lambda b, t: (b, 0, 0)),
                  _const_spec((half, D_MODEL)), _const_spec((CONV_CH, D_MODEL)),
                  _const_spec((1, D_MODEL)),
                  _const_spec((2 * FFN_SLABS, D_MODEL, FFN_SLAB)),
                  _const_spec((2 * FFN_SLABS, FFN_CONV_WIDTH, FFN_SLAB)),
                  _const_spec((2 * FFN_SLABS, 1, FFN_SLAB)),
                  _const_spec((FFN_SLABS, FFN_SLAB, D_MODEL)),
                  st, _const_spec((1, D_MODEL))],
        out_specs=[tile(D_MODEL), st],
        out_shape=[jax.ShapeDtypeStruct((bsz, seq, D_MODEL), F32),
                   jax.ShapeDtypeStruct((bsz, 2 * FFN_SLABS, hist, FFN_SLAB), F32)],
        scratch_shapes=[pltpu.VMEM((tm, D_MODEL), BF16), pltpu.VMEM((tm, D_MODEL), F32),
                        pltpu.VMEM((2 * FFN_SLABS, hist, FFN_SLAB), F32)],
        compiler_params=pltpu.CompilerParams(
            dimension_semantics=("arbitrary", "arbitrary"), vmem_limit_bytes=VMEM_LIMIT),
        name="mix_out_ffn",
    )(x, a_out, b_out, ada, p["w_out_a"], p["w_out_b"], p["norm2_g"], p["ffn_up"], p["ffn_conv_w"],
      p["ffn_conv_b"], p["ffn_down"], past, final_g)


def _layer_params(l, norm1_g, w_in, mu_shift, w0, w2, a0, a2, g2, k_k, k_a, r_k, lnx_w, lnx_b, conv_w,
                  conv_b, cln_w, cln_b, w_out, norm2_g, ffn_up, ffn_conv_w, ffn_conv_b, ffn_down):
    row = lambda t: t[l].reshape(1, -1)
    zeros = jnp.zeros((DECAY_LORA, RWKV_WIDTH), F32)
    lora = jnp.concatenate([jnp.concatenate([w2[l], zeros], axis=1),
                            jnp.concatenate([zeros, a2[l]], axis=1)], axis=0)
    slabs = 2 * FFN_SLABS
    return dict(
        norm1_g=row(norm1_g), w_in=w_in[l].astype(BF16), mu_shift=row(mu_shift), w0=row(w0),
        lora=lora.astype(BF16), a0=row(a0), g2=g2[l].astype(BF16), k_k=row(k_k), k_a=row(k_a),
        r_k=row(r_k), lnx_w=row(lnx_w), lnx_b=row(lnx_b), conv_w=conv_w[l], conv_b=row(conv_b),
        cln_w=row(cln_w), cln_b=row(cln_b),
        w_out_a=w_out[l, :RWKV_WIDTH].astype(BF16), w_out_b=w_out[l, RWKV_WIDTH:].astype(BF16),
        norm2_g=row(norm2_g),
        ffn_up=ffn_up[l].astype(BF16).reshape(D_MODEL, slabs, FFN_SLAB).transpose(1, 0, 2),
        ffn_conv_w=ffn_conv_w[l].reshape(FFN_CONV_WIDTH, slabs, FFN_SLAB).transpose(1, 0, 2),
        ffn_conv_b=ffn_conv_b[l].reshape(slabs, 1, FFN_SLAB),
        ffn_down=ffn_down[l].astype(BF16).reshape(FFN_SLABS, FFN_SLAB, D_MODEL),
    )


def _trunk(x, ada, st_wkv, st_shift, st_conv, st_ffn, params, vres, final_g):
    bsz, seq, _ = x.shape
    tm = min(MAX_TIME_TILE, seq)
    chunk = min(MAX_CHUNK, seq)
    hist = FFN_CONV_WIDTH - 1
    slabs = 2 * FFN_SLABS
    v_first = None
    new = ([], [], [], [])
    for l in range(DEPTH):
        p = params[l]
        r, lw, k, v, kk, a, g, glu, n_shift = _mix_in_call(
            x, ada[l], st_shift[l], p, None if l == 0 else vres[l - 1], v_first, tm)
        if l == 0:
            v_first = v
        a_out, s_bd = _wkv_call(r, lw, k, v, kk, a, g, _state_to_blockdiag(st_wkv[l]), p, tm, chunk)
        b_out, n_conv = _conv_call(glu, st_conv[l], p, tm)
        past = st_ffn[l].reshape(bsz, hist, slabs, FFN_SLAB).transpose(0, 2, 1, 3)
        x, n_ffn = _ffn_call(x, a_out, b_out, ada[l], past, p, final_g, l == DEPTH - 1, tm)
        new[0].append(_blockdiag_to_state(s_bd))
        new[1].append(n_shift)
        new[2].append(n_conv)
        new[3].append(n_ffn.transpose(0, 2, 1, 3).reshape(bsz, hist, 2 * D_FF))
    return x, [jnp.stack(t) for t in new]


def kernel(x_prompt, x_sample, c_prompt, c_sample, state_wkv, state_shift, state_conv, state_ffn_conv,
           norm1_g, ada_w, ada_b, w_in, mu_shift, w0, w2, a0, a2, g2, k_k, k_a, r_k, v0, v1, v2,
           lnx_w, lnx_b, conv_w, conv_b, cln_w, cln_b, w_out, norm2_g, ffn_up, ffn_conv_w, ffn_conv_b,
           ffn_down, final_g):
    bp = x_prompt.shape[0]
    bs = x_sample.shape[0]
    params = [_layer_params(l, norm1_g, w_in, mu_shift, w0, w2, a0, a2, g2, k_k, k_a, r_k, lnx_w, lnx_b,
                            conv_w, conv_b, cln_w, cln_b, w_out, norm2_g, ffn_up, ffn_conv_w,
                            ffn_conv_b, ffn_down) for l in range(DEPTH)]
    vres = [(v0[l].reshape(1, -1), v1[l].astype(BF16), v2[l].astype(BF16)) for l in range(DEPTH - 1)]
    fg = final_g.reshape(1, -1)

    c_all = jnp.concatenate([c_prompt, c_sample], axis=0)
    pad = (-c_all.shape[0]) % 8
    c_all = jnp.concatenate([c_all, jnp.zeros((pad, D_MODEL), F32)], axis=0)
    ada = _ada_call(c_all, ada_w, ada_b).reshape(DEPTH, -1, 6, D_MODEL)
    ada_p, ada_s = ada[:, :bp], ada[:, bp:bp + bs]

    z_wkv = jnp.zeros((DEPTH, bp, RWKV_HEADS, HEAD_SIZE, HEAD_SIZE), F32)
    z_shift = jnp.zeros((DEPTH, bp, 1, RWKV_COLS), F32)
    z_conv = jnp.zeros((DEPTH, bp, CONV_WIDTH - 1, CONV_CH), F32)
    z_ffn = jnp.zeros((DEPTH, bp, FFN_CONV_WIDTH - 1, 2 * D_FF), F32)
    y_p, (wkv_p, shift_p, conv_p, ffn_p) = _trunk(
        x_prompt, ada_p, z_wkv, z_shift, z_conv, z_ffn, params, vres, fg)
    y_s, (wkv_s, shift_s, conv_s, ffn_s) = _trunk(
        x_sample, ada_s, state_wkv, state_shift, state_conv, state_ffn_conv, params, vres, fg)
    return (y_p, y_s, wkv_p, shift_p, conv_p, ffn_p, wkv_s, shift_s, conv_s, ffn_s)
```

```python
import functools

import jax
import jax.numpy as jnp
from jax import lax
from jax.experimental import pallas as pl
from jax.experimental.pallas import tpu as pltpu

D_MODEL = 1024
DEPTH = 2
RWKV_WIDTH = 512
HEAD_SIZE = 64
RWKV_HEADS = RWKV_WIDTH // HEAD_SIZE
DECAY_LORA = 64
ICLR_LORA = 64
GATE_LORA = 128
RWKV_COLS = 3 * RWKV_WIDTH + DECAY_LORA + ICLR_LORA + GATE_LORA
CONV_CH = 512
CONV_WIDTH = 31
IN_COLS = RWKV_COLS + 2 * CONV_CH
D_FF = 2816
FFN_CONV_WIDTH = 3
NORM_EPS = 1e-6
LN_EPS = 1e-5
GN_EPS = 64e-5

MXU_WIDTH = 256
HEADS_PER_GROUP = MXU_WIDTH // HEAD_SIZE
HEAD_GROUPS = RWKV_HEADS // HEADS_PER_GROUP
FFN_SLAB = MXU_WIDTH
FFN_SLABS = D_FF // FFN_SLAB
MAX_CHUNK = 64
MAX_TIME_TILE = 512
CONV_HALO = 32
CONV_ROW_BLOCK = 32
CONV_COPY_ROWS = 64
SUBLANES = 8
VMEM_LIMIT = 56 * 1024 * 1024
INTRA_PASSES = 1
STATE_PASSES = 1
WKV_LOCKSTEP_CHUNKS = 4
INTRA_PASSES_PARTS = 1 if INTRA_PASSES == 1 else 2
STATE_PASSES_PARTS = 1 if STATE_PASSES == 1 else 2

F32 = jnp.float32
BF16 = jnp.bfloat16


def _dot(a, b):
    return jnp.dot(a, b, preferred_element_type=F32)


def _dot_nt(a, b):
    return lax.dot_general(a, b, (((1,), (1,)), ((), ())), preferred_element_type=F32)


def _split2(x):
    hi = x.astype(BF16)
    lo = (x - hi.astype(F32)).astype(BF16)
    return hi, lo


def _head_sum(x, seg):
    hi, lo = _split2(x)
    return _dot(hi, seg) + _dot(lo, seg)


def _log2(n):
    assert n & (n - 1) == 0
    return n.bit_length() - 1


def _div(x, n):
    return x >> _log2(n)


def _mod(x, n):
    return x & (n - 1)


def _head_seg_matrix(width):
    row = _div(lax.broadcasted_iota(jnp.int32, (width, width), 0), HEAD_SIZE)
    col = _div(lax.broadcasted_iota(jnp.int32, (width, width), 1), HEAD_SIZE)
    return (row == col).astype(BF16)


def _sigmoid(x):
    return 1.0 / (1.0 + jnp.exp(-x))


def _silu(x):
    return x * _sigmoid(x)


def _rms_norm(x, g):
    return x * lax.rsqrt(jnp.mean(x * x, axis=-1, keepdims=True) + NORM_EPS) * g


def _shift_rows(x, carry_rows, shift):
    rows = lax.broadcasted_iota(jnp.int32, x.shape, 0)
    out = pltpu.roll(x, shift, 0)
    n_carry = carry_rows.shape[0]
    for i in range(shift):
        out = jnp.where(rows == i, carry_rows[n_carry - shift + i:n_carry - shift + i + 1], out)
    return out


def _ada_kernel(c_ref, w_ref, b_ref, o_ref):
    c = c_ref[...]
    o_ref[0] = _dot(_silu(c).astype(BF16), w_ref[0].astype(BF16)) + b_ref[0]


def _ada_call(c_all, ada_w, ada_b):
    rows = c_all.shape[0]
    n_tile = 1536
    return pl.pallas_call(
        _ada_kernel,
        grid=(DEPTH, 6 * D_MODEL // n_tile),
        in_specs=[
            pl.BlockSpec((rows, D_MODEL), lambda l, n: (0, 0)),
            pl.BlockSpec((1, D_MODEL, n_tile), lambda l, n: (l, 0, n)),
            pl.BlockSpec((1, 1, n_tile), lambda l, n: (l, 0, n)),
        ],
        out_specs=pl.BlockSpec((1, rows, n_tile), lambda l, n: (l, 0, n)),
        out_shape=jax.ShapeDtypeStruct((DEPTH, rows, 6 * D_MODEL), F32),
        compiler_params=pltpu.CompilerParams(
            dimension_semantics=("arbitrary", "arbitrary"), vmem_limit_bytes=VMEM_LIMIT),
        name="ada_vectors",
    )(c_all, ada_w, ada_b.reshape(DEPTH, 1, 6 * D_MODEL))


def _mix_in_kernel(has_vres, tm, *refs):
    (x_ref, ada_ref, g1_ref, win_ref, sh0_ref, mu_ref, w0_ref, lora_ref, a0_ref, g2_ref,
     kk_ref, ka_ref) = refs[:12]
    refs = refs[12:]
    if has_vres:
        v0_ref, v1_ref, v2_ref, vf_ref = refs[:4]
        refs = refs[4:]
    r_o, lw_o, k_o, v_o, kkn_o, a_o, g_o, glu_o, shift_o, carry = refs

    @pl.when(pl.program_id(1) == 0)
    def _():
        carry[...] = sh0_ref[0]

    x = x_ref[0]
    ada = ada_ref[0]
    h = _rms_norm(x, g1_ref[...]) * (1.0 + ada[1:2]) + ada[0:1]
    proj = _dot(h.astype(BF16), win_ref[...])
    rw = proj[:, :RWKV_COLS]
    cv = proj[:, RWKV_COLS:]
    glu_o[0] = cv[:, :CONV_CH] * _sigmoid(cv[:, CONV_CH:])

    last = rw[tm - 1:tm]
    shift_o[0] = last
    prev = _shift_rows(rw, carry[...], 1)
    carry[...] = last
    xs = rw + (prev - rw) * mu_ref[...]

    w = RWKV_WIDTH
    xr, xk, xv = xs[:, :w], xs[:, w:2 * w], xs[:, 2 * w:3 * w]
    x_lora = xs[:, 3 * w:3 * w + DECAY_LORA + ICLR_LORA]
    xg = xs[:, 3 * w + DECAY_LORA + ICLR_LORA:]
    lane = lax.broadcasted_iota(jnp.int32, x_lora.shape, 1)
    lora_in = jnp.where(lane < DECAY_LORA, jnp.tanh(x_lora), x_lora)
    lora = _dot(lora_in.astype(BF16), lora_ref[...])

    z = -(w0_ref[...] + lora[:, :w])
    softplus = jnp.maximum(z, 0.0) + jnp.log(1.0 + jnp.exp(-jnp.abs(z)))
    lw_o[0] = -jnp.exp(-softplus - 0.5)
    a = _sigmoid(a0_ref[...] + lora[:, w:])
    a_o[0] = a
    g_o[0] = _dot(_sigmoid(xg).astype(BF16), g2_ref[...])

    if has_vres:
        gate = _dot(_dot(xv.astype(BF16), v1_ref[...]).astype(BF16), v2_ref[...])
        v_o[0] = xv + (vf_ref[0] - xv) * _sigmoid(v0_ref[...] + gate)
    else:
        v_o[0] = xv

    kk = xk * kk_ref[...]
    norm = jnp.sqrt(_head_sum(kk * kk, _head_seg_matrix(w)))
    kkn_o[0] = kk / jnp.maximum(norm, 1e-12)
    k_o[0] = xk * (1.0 + (a - 1.0) * ka_ref[...])
    r_o[0] = xr


def _const_spec(shape):
    return pl.BlockSpec(shape, lambda b, t: (0,) * len(shape), pipeline_mode=pl.Buffered(1))


def _mix_in_call(x, ada, shift0, p, vres, v_first, tm):
    bsz, seq, _ = x.shape
    w = RWKV_WIDTH
    tile = lambda width: pl.BlockSpec((1, tm, width), lambda b, t: (b, t, 0))
    per_batch = lambda rows, width: pl.BlockSpec((1, rows, width), lambda b, t: (b, 0, 0))
    args = [x, ada, p["norm1_g"], p["w_in"], shift0, p["mu_shift"], p["w0"], p["lora"], p["a0"],
            p["g2"], p["k_k"], p["k_a"]]
    in_specs = [tile(D_MODEL), per_batch(6, D_MODEL), _const_spec((1, D_MODEL)),
                _const_spec((D_MODEL, IN_COLS)), per_batch(1, RWKV_COLS), _const_spec((1, RWKV_COLS)),
                _const_spec((1, w)), _const_spec((DECAY_LORA + ICLR_LORA, 2 * w)), _const_spec((1, w)),
                _const_spec((GATE_LORA, w)), _const_spec((1, w)), _const_spec((1, w))]
    if vres is not None:
        v0, v1, v2 = vres
        args += [v0, v1, v2, v_first]
        in_specs += [_const_spec((1, w)), _const_spec(v1.shape), _const_spec(v2.shape), tile(w)]
    act = jax.ShapeDtypeStruct((bsz, seq, w), F32)
    return pl.pallas_call(
        functools.partial(_mix_in_kernel, vres is not None, tm),
        grid=(bsz, seq // tm),
        in_specs=in_specs,
        out_specs=[tile(w)] * 8 + [per_batch(1, RWKV_COLS)],
        out_shape=[act] * 8 + [jax.ShapeDtypeStruct((bsz, 1, RWKV_COLS), F32)],
        scratch_shapes=[pltpu.VMEM((1, RWKV_COLS), F32)],
        compiler_params=pltpu.CompilerParams(
            dimension_semantics=("arbitrary", "arbitrary"), vmem_limit_bytes=VMEM_LIMIT),
        name="mix_in",
    )(*args)


def _parts(x, passes):
    return (x.astype(BF16),) if passes == 1 else _split2(x)


def _mm(a_parts, b_parts, dot=_dot):
    out = dot(a_parts[0], b_parts[0])
    if len(a_parts) > 1:
        out = out + dot(a_parts[1], b_parts[0]) + dot(a_parts[0], b_parts[1])
    return out


def _wkv_kernel(tm, chunk, r_ref, lw_ref, k_ref, v_ref, kk_ref, a_ref, g_ref, s0_ref, rk_ref,
                lnw_ref, lnb_ref, out_ref, sn_ref, state, lhs_sc, q_sc, qt_sc, rkv_sc, arb_sc,
                bdec_sc, vk_sc, gl_sc):
    n_heads = HEADS_PER_GROUP
    gw = MXU_WIDTH
    side = n_heads * chunk
    n_chunks = tm // chunk
    lockstep = min(WKV_LOCKSTEP_CHUNKS, n_chunks)

    @pl.when(pl.program_id(1) == 0)
    def _():
        state[...] = s0_ref[0]

    def iota(shape, dim):
        return lax.broadcasted_iota(jnp.int32, shape, dim)

    mask_data = _div(iota((side, gw), 0), chunk) == _div(iota((side, gw), 1), HEAD_SIZE)
    mask_side = _div(iota((side, side), 0), chunk) == _div(iota((side, side), 1), chunk)
    mask_state = _div(iota((gw, gw), 0), HEAD_SIZE) == _div(iota((gw, gw), 1), HEAD_SIZE)
    t_idx = iota((chunk, side), 0)
    s_idx = _mod(iota((chunk, side), 1), chunk)
    strict = s_idx < t_idx
    incl = s_idx <= t_idx
    eye = (s_idx == t_idx).astype(F32)
    row_idx = iota((chunk, RWKV_WIDTH), 0)

    def expand(parts, mask):
        zero = jnp.zeros((), BF16)
        return tuple(jnp.where(mask, jnp.concatenate([q] * n_heads, axis=0), zero) for q in parts)

    def expand_data(x, passes=INTRA_PASSES):
        return expand(_parts(x, passes), mask_data)

    def expand_side(x):
        return expand(_parts(x, INTRA_PASSES), mask_side)

    def intra(x):
        return _parts(x, INTRA_PASSES)

    def on_state(x):
        return _parts(x, STATE_PASSES)

    def lower_left(blk):
        return ((_div(t_idx, 2 * blk) == _div(s_idx, 2 * blk))
                & (_mod(_div(t_idx, blk), 2) == 1) & (_mod(_div(s_idx, blk), 2) == 0))

    def precompute(step_idx, carry):
        chains = []
        for j in range(lockstep):
            c = step_idx * lockstep + j
            rows = pl.ds(pl.multiple_of(c * chunk, chunk), chunk)
            r = r_ref[0, rows, :]
            lw = lw_ref[0, rows, :]
            k = k_ref[0, rows, :]
            v = v_ref[0, rows, :]
            kk = kk_ref[0, rows, :]
            a = a_ref[0, rows, :]

            cum = lw
            step = 1
            while step < chunk:
                cum = cum + jnp.where(row_idx >= step, pltpu.roll(cum, step, 0), 0.0)
                step *= 2
            total = cum[chunk - 1:chunk]
            decay_inv = jnp.exp(-cum)
            decay_out = jnp.exp(total - cum)
            gl_sc[c] = jnp.exp(total)

            bhat = kk * a
            qa = jnp.concatenate([-kk * jnp.exp(cum - lw), r * jnp.exp(cum)], axis=0)
            kb_b = bhat * decay_inv
            kb_k = k * decay_inv
            bdec = bhat * decay_out
            kdec = k * decay_out
            for grp in range(HEAD_GROUPS):
                lanes = slice(grp * gw, (grp + 1) * gw)
                chains.append(dict(c=c, grp=grp, qa=qa[:, lanes], v=v[:, lanes], kb_b=kb_b[:, lanes],
                                   kb_k=kb_k[:, lanes], bdec=bdec[:, lanes], kdec=kdec[:, lanes]))

        for ch in chains:
            qa_p = intra(ch["qa"])
            a_b = _mm(qa_p, expand_data(ch["kb_b"]), _dot_nt)
            a_k = _mm(qa_p, expand_data(ch["kb_k"]), _dot_nt)
            ch["a_ab"] = jnp.where(strict, a_b[:chunk], 0.0)
            ch["a_rb"] = jnp.where(incl, a_b[chunk:], 0.0)
            ch["a_k"] = jnp.where(jnp.concatenate([strict, incl], axis=0), a_k, 0.0)
            ch["tinv"] = eye + jnp.where(lower_left(1), ch["a_ab"], 0.0)
        blk = 2
        while blk < chunk:
            for ch in chains:
                ch["inner"] = _mm(intra(jnp.where(lower_left(blk), ch["a_ab"], 0.0)),
                                  expand_side(ch["tinv"]))
            for ch in chains:
                ch["tinv"] = ch["tinv"] + _mm(intra(ch["tinv"]), expand_side(ch["inner"]))
            blk *= 2

        for ch in chains:
            ch["tinv_p"] = intra(ch["tinv"])
            ch["kv"] = _mm(intra(ch["a_k"]), expand_data(ch["v"]))
            ch["pmat"] = _mm(ch["tinv_p"], expand_data(ch["qa"][:chunk]))
            vk = _mm(on_state(ch["v"].T), on_state(ch["kdec"]))
            vk_sc[ch["c"], ch["grp"]] = jnp.where(mask_state, vk, 0.0)
        for ch in chains:
            c, grp = ch["c"], ch["grp"]
            q = _mm(ch["tinv_p"], expand_data(ch["kv"][:chunk]))
            q_sc[c, grp] = q
            qt_sc[c, grp] = q.T
            rkv_sc[c, grp] = ch["kv"][chunk:]
            lhs = jnp.concatenate([ch["pmat"], ch["qa"][chunk:]], axis=0)
            for i, part in enumerate(on_state(lhs)):
                lhs_sc[c, grp, i] = part
            for i, part in enumerate(on_state(ch["bdec"])):
                bdec_sc[c, grp, i] = part
            for i, part in enumerate(intra(ch["a_rb"])):
                arb_sc[c, grp, i] = part
        return carry

    lax.fori_loop(0, n_chunks // lockstep, precompute, 0)

    def lhs_parts(c, grp, n_rows):
        return tuple(lhs_sc[c, grp, i, 0:n_rows, :] for i in range(STATE_PASSES_PARTS))

    def outputs_first_half(c, s_parts):
        return [_mm(lhs_parts(c, grp, 2 * chunk), s_parts[grp], _dot_nt)
                for grp in range(HEAD_GROUPS)]

    def outputs_second_half(c, xu):
        for grp in range(HEAD_GROUPS):
            lanes = slice(grp * gw, (grp + 1) * gw)
            u = xu[grp][:chunk] + q_sc[c, grp]
            arb = tuple(arb_sc[c, grp, i] for i in range(INTRA_PASSES_PARTS))
            out_ref[0, c * chunk:(c + 1) * chunk, lanes] = (
                xu[grp][chunk:] + rkv_sc[c, grp] + _mm(arb, expand_data(u)))

    late = None
    for c in range(n_chunks):
        s_old = [state[grp] for grp in range(HEAD_GROUPS)]
        s_parts = [on_state(s) for s in s_old]
        u_t = [_mm(s_parts[grp], lhs_parts(c, grp, chunk), _dot_nt) + qt_sc[c, grp]
               for grp in range(HEAD_GROUPS)]
        xu_late = outputs_first_half(*late) if late is not None else None
        decay_all = gl_sc[c]
        for grp in range(HEAD_GROUPS):
            lanes = slice(grp * gw, (grp + 1) * gw)
            bd = tuple(bdec_sc[c, grp, i] for i in range(STATE_PASSES_PARTS))
            upd = _mm(on_state(u_t[grp]), bd)
            state[grp] = (s_old[grp] * decay_all[:, lanes] + vk_sc[c, grp]
                          + jnp.where(mask_state, upd, 0.0))
        if late is not None:
            outputs_second_half(late[0], xu_late)
        late = (c, s_parts)
    outputs_second_half(late[0], outputs_first_half(*late))
    sn_ref[0] = state[...]

    seg = _head_seg_matrix(gw)

    def head_sum(x):
        hi, lo = _split2(jnp.concatenate([x[:, :gw], x[:, gw:]], axis=0))
        s = _dot(jnp.concatenate([hi, lo], axis=0), seg)
        s = s[:2 * tm] + s[2 * tm:]
        return jnp.concatenate([s[:tm], s[tm:]], axis=1)

    y = out_ref[0]
    inv_n = 1.0 / HEAD_SIZE
    cen = y - head_sum(y) * inv_n
    var = head_sum(cen * cen) * inv_n
    yn = cen * lax.rsqrt(var + GN_EPS) * lnw_ref[...] + lnb_ref[...]
    bonus = head_sum(r_ref[0] * k_ref[0] * rk_ref[...]) * v_ref[0]
    out_ref[0] = (yn + bonus) * g_ref[0]


def _wkv_call(r, lw, k, v, kk, a, g, s0_bd, p, tm, chunk):
    bsz, seq, w = r.shape
    tile = pl.BlockSpec((1, tm, w), lambda b, t: (b, t, 0))
    st = pl.BlockSpec((1, HEAD_GROUPS, MXU_WIDTH, MXU_WIDTH), lambda b, t: (b, 0, 0, 0))
    vec = _const_spec((1, w))
    gw = MXU_WIDTH
    side = HEADS_PER_GROUP * chunk
    n_chunks = tm // chunk
    return pl.pallas_call(
        functools.partial(_wkv_kernel, tm, chunk),
        grid=(bsz, seq // tm),
        in_specs=[tile] * 7 + [st, vec, vec, vec],
        out_specs=[tile, st],
        out_shape=[jax.ShapeDtypeStruct((bsz, seq, w), F32),
                   jax.ShapeDtypeStruct((bsz, HEAD_GROUPS, MXU_WIDTH, MXU_WIDTH), F32)],
        scratch_shapes=[
            pltpu.VMEM((HEAD_GROUPS, gw, gw), F32),
            pltpu.VMEM((n_chunks, HEAD_GROUPS, STATE_PASSES_PARTS, 2 * chunk, gw), BF16),
            pltpu.VMEM((n_chunks, HEAD_GROUPS, chunk, gw), F32),
            pltpu.VMEM((n_chunks, HEAD_GROUPS, gw, chunk), F32),
            pltpu.VMEM((n_chunks, HEAD_GROUPS, chunk, gw), F32),
            pltpu.VMEM((n_chunks, HEAD_GROUPS, INTRA_PASSES_PARTS, chunk, side), BF16),
            pltpu.VMEM((n_chunks, HEAD_GROUPS, STATE_PASSES_PARTS, chunk, gw), BF16),
            pltpu.VMEM((n_chunks, HEAD_GROUPS, gw, gw), F32),
            pltpu.VMEM((n_chunks, 1, w), F32),
        ],
        compiler_params=pltpu.CompilerParams(
            dimension_semantics=("arbitrary", "arbitrary"), vmem_limit_bytes=VMEM_LIMIT),
        name="wkv7_chunked",
    )(r, lw, k, v, kk, a, g, s0_bd, p["r_k"], p["lnx_w"], p["lnx_b"])


def _state_to_blockdiag(s):
    bsz = s.shape[0]
    s = s.reshape(bsz, HEAD_GROUPS, HEADS_PER_GROUP, HEAD_SIZE, HEAD_SIZE)
    eye = jnp.eye(HEADS_PER_GROUP, dtype=s.dtype)
    out = jnp.einsum("bgivk,ij->bgivjk", s, eye)
    return out.reshape(bsz, HEAD_GROUPS, MXU_WIDTH, MXU_WIDTH)


def _blockdiag_to_state(s_bd):
    bsz = s_bd.shape[0]
    s = s_bd.reshape(bsz, HEAD_GROUPS, HEADS_PER_GROUP, HEAD_SIZE, HEADS_PER_GROUP, HEAD_SIZE)
    idx = jnp.arange(HEADS_PER_GROUP)
    s = s[:, :, idx, :, idx, :]
    s = jnp.moveaxis(s, 0, 2)
    return s.reshape(bsz, RWKV_HEADS, HEAD_SIZE, HEAD_SIZE)


def _conv_kernel(tm, glu_ref, past_ref, w_ref, b_ref, lnw_ref, lnb_ref, out_ref, new_ref, buf,
                 shifted):
    hist = CONV_WIDTH - 1
    pad = CONV_HALO - hist
    span = CONV_HALO + tm

    @pl.when(pl.program_id(1) == 0)
    def _():
        buf[0:pad, :] = jnp.zeros((pad, CONV_CH), F32)
        buf[pad:CONV_HALO, :] = past_ref[0]

    buf[CONV_HALO:span, :] = glu_ref[0]
    n_shifted = span - SUBLANES
    for s in range(1, SUBLANES):
        for r0 in range(0, n_shifted, CONV_COPY_ROWS):
            n = min(CONV_COPY_ROWS, n_shifted - r0)
            shifted[s - 1, r0:r0 + n, :] = buf[r0 + s:r0 + s + n, :]

    blk = min(CONV_ROW_BLOCK, tm)
    for rb in range(tm // blk):
        acc = jnp.zeros((blk, CONV_CH), F32) + b_ref[...]
        for j in range(CONV_WIDTH):
            q, s = divmod(j + pad, SUBLANES)
            r0 = rb * blk + q * SUBLANES
            rows = buf[r0:r0 + blk, :] if s == 0 else shifted[s - 1, r0:r0 + blk, :]
            acc = acc + w_ref[j:j + 1, :] * rows
        mean = jnp.mean(acc, axis=-1, keepdims=True)
        cen = acc - mean
        var = jnp.mean(cen * cen, axis=-1, keepdims=True)
        yn = cen * lax.rsqrt(var + LN_EPS) * lnw_ref[...] + lnb_ref[...]
        out_ref[0, rb * blk:(rb + 1) * blk, :] = _silu(yn)
    new_ref[0] = buf[tm + pad:tm + CONV_HALO, :]
    buf[0:CONV_HALO, :] = buf[tm:tm + CONV_HALO, :]


def _conv_call(glu, past, p, tm):
    bsz, seq, ch = glu.shape
    hist = CONV_WIDTH - 1
    tile = pl.BlockSpec((1, tm, ch), lambda b, t: (b, t, 0))
    st = pl.BlockSpec((1, hist, ch), lambda b, t: (b, 0, 0))
    vec = _const_spec((1, ch))
    return pl.pallas_call(
        functools.partial(_conv_kernel, tm),
        grid=(bsz, seq // tm),
        in_specs=[tile, st, _const_spec((CONV_WIDTH, ch)), vec, vec, vec],
        out_specs=[tile, st],
        out_shape=[jax.ShapeDtypeStruct((bsz, seq, ch), F32),
                   jax.ShapeDtypeStruct((bsz, hist, ch), F32)],
        scratch_shapes=[pltpu.VMEM((CONV_HALO + tm, ch), F32),
                        pltpu.VMEM((SUBLANES - 1, CONV_HALO + tm - SUBLANES, ch), F32)],
        compiler_params=pltpu.CompilerParams(
            dimension_semantics=("arbitrary", "arbitrary"), vmem_limit_bytes=VMEM_LIMIT),
        name="conformer_conv",
    )(glu, past, p["conv_w"], p["conv_b"], p["cln_w"], p["cln_b"])


def _ffn_kernel(final, tm, x_ref, a_ref, b_ref, ada_ref, woa_ref, wob_ref, g2_ref, up_ref, cw_ref,
                cb_ref, down_ref, past_ref, fg_ref, out_ref, new_ref, h2_sc, acc_sc, carry):
    @pl.when(pl.program_id(1) == 0)
    def _():
        carry[...] = past_ref[0]

    ada = ada_ref[0]
    mix = _dot(a_ref[0].astype(BF16), woa_ref[...]) + _dot(b_ref[0].astype(BF16), wob_ref[...])
    x1 = x_ref[0] + ada[2:3] * mix
    h2 = _rms_norm(x1, g2_ref[...]) * (1.0 + ada[4:5]) + ada[3:4]
    h2_sc[...] = h2.astype(BF16)
    acc_sc[...] = jnp.zeros((tm, D_MODEL), F32)

    def cols(j):
        return slice(j * FFN_SLAB, (j + 1) * FFN_SLAB)

    def up_proj(j):
        return (_dot(h2_sc[...], up_ref[:, cols(j)]),
                _dot(h2_sc[...], up_ref[:, cols(j + FFN_SLABS)]))

    def causal_conv(u, j):
        past = carry[:, cols(j)]
        cw = cw_ref[:, cols(j)]
        y = (cw[0:1] * _shift_rows(u, past, 2) + cw[1:2] * _shift_rows(u, past, 1)
             + cw[2:3] * u + cb_ref[:, cols(j)])
        carry[:, cols(j)] = u[tm - (FFN_CONV_WIDTH - 1):tm]
        return y

    ahead = up_proj(0)
    for j in range(FFN_SLABS):
        u_gate, u_val = ahead
        if j + 1 < FFN_SLABS:
            ahead = up_proj(j + 1)
        act = _silu(causal_conv(u_gate, j)) * causal_conv(u_val, j + FFN_SLABS)
        acc_sc[...] += _dot(act.astype(BF16), down_ref[cols(j), :])
    x2 = x1 + ada[5:6] * acc_sc[...]
    out_ref[0] = _rms_norm(x2, fg_ref[...]) if final else x2
    new_ref[0] = carry[...]


def _ffn_call(x, a_out, b_out, ada, past, p, final_g, final, tm):
    bsz, seq, _ = x.shape
    hist = FFN_CONV_WIDTH - 1
    tile = lambda width: pl.BlockSpec((1, tm, width), lambda b, t: (b, t, 0))
    st = pl.BlockSpec((1, hist, 2 * D_FF), lambda b, t: (b, 0, 0))
    half = RWKV_WIDTH
    return pl.pallas_call(
        functools.partial(_ffn_kernel, final, tm),
        grid=(bsz, seq // tm),
        in_specs=[tile(D_MODEL), tile(half), tile(CONV_CH),
                  pl.BlockSpec((1, 6, D_MODEL), lambda b, t: (b, 0, 0)),
                  _const_spec((half, D_MODEL)), _const_spec((CONV_CH, D_MODEL)),
                  _const_spec((1, D_MODEL)),
                  _const_spec((D_MODEL, 2 * D_FF)),
                  _const_spec((FFN_CONV_WIDTH, 2 * D_FF)),
                  _const_spec((1, 2 * D_FF)),
                  _const_spec((D_FF, D_MODEL)),
                  st, _const_spec((1, D_MODEL))],
        out_specs=[tile(D_MODEL), st],
        out_shape=[jax.ShapeDtypeStruct((bsz, seq, D_MODEL), F32),
                   jax.ShapeDtypeStruct((bsz, hist, 2 * D_FF), F32)],
        scratch_shapes=[pltpu.VMEM((tm, D_MODEL), BF16), pltpu.VMEM((tm, D_MODEL), F32),
                        pltpu.VMEM((hist, 2 * D_FF), F32)],
        compiler_params=pltpu.CompilerParams(
            dimension_semantics=("arbitrary", "arbitrary"), vmem_limit_bytes=VMEM_LIMIT),
        name="mix_out_ffn",
    )(x, a_out, b_out, ada, p["w_out_a"], p["w_out_b"], p["norm2_g"], p["ffn_up"], p["ffn_conv_w"],
      p["ffn_conv_b"], p["ffn_down"], past, final_g)


def _layer_params(l, norm1_g, w_in, mu_shift, w0, w2, a0, a2, g2, k_k, k_a, r_k, lnx_w, lnx_b, conv_w,
                  conv_b, cln_w, cln_b, w_out, norm2_g, ffn_up, ffn_conv_w, ffn_conv_b, ffn_down):
    row = lambda t: t[l].reshape(1, -1)
    zeros = jnp.zeros((DECAY_LORA, RWKV_WIDTH), F32)
    lora = jnp.concatenate([jnp.concatenate([w2[l], zeros], axis=1),
                            jnp.concatenate([zeros, a2[l]], axis=1)], axis=0)
    return dict(
        norm1_g=row(norm1_g), w_in=w_in[l].astype(BF16), mu_shift=row(mu_shift), w0=row(w0),
        lora=lora.astype(BF16), a0=row(a0), g2=g2[l].astype(BF16), k_k=row(k_k), k_a=row(k_a),
        r_k=row(r_k), lnx_w=row(lnx_w), lnx_b=row(lnx_b), conv_w=conv_w[l], conv_b=row(conv_b),
        cln_w=row(cln_w), cln_b=row(cln_b),
        w_out_a=w_out[l, :RWKV_WIDTH].astype(BF16), w_out_b=w_out[l, RWKV_WIDTH:].astype(BF16),
        norm2_g=row(norm2_g),
        ffn_up=ffn_up[l].astype(BF16), ffn_conv_w=ffn_conv_w[l], ffn_conv_b=row(ffn_conv_b),
        ffn_down=ffn_down[l].astype(BF16),
    )


def _trunk(x, ada, st_wkv, st_shift, st_conv, st_ffn, params, vres, final_g):
    bsz, seq, _ = x.shape
    tm = min(MAX_TIME_TILE, seq)
    chunk = min(MAX_CHUNK, seq)
    v_first = None
    new = ([], [], [], [])
    for l in range(DEPTH):
        p = params[l]
        r, lw, k, v, kk, a, g, glu, n_shift = _mix_in_call(
            x, ada[l], st_shift[l], p, None if l == 0 else vres[l - 1], v_first, tm)
        if l == 0:
            v_first = v
        a_out, s_bd = _wkv_call(r, lw, k, v, kk, a, g, _state_to_blockdiag(st_wkv[l]), p, tm, chunk)
        b_out, n_conv = _conv_call(glu, st_conv[l], p, tm)
        x, n_ffn = _ffn_call(x, a_out, b_out, ada[l], st_ffn[l], p, final_g, l == DEPTH - 1, tm)
        new[0].append(_blockdiag_to_state(s_bd))
        new[1].append(n_shift)
        new[2].append(n_conv)
        new[3].append(n_ffn)
    return x, [jnp.stack(t) for t in new]


def kernel(x_prompt, x_sample, c_prompt, c_sample, state_wkv, state_shift, state_conv, state_ffn_conv,
           norm1_g, ada_w, ada_b, w_in, mu_shift, w0, w2, a0, a2, g2, k_k, k_a, r_k, v0, v1, v2,
           lnx_w, lnx_b, conv_w, conv_b, cln_w, cln_b, w_out, norm2_g, ffn_up, ffn_conv_w, ffn_conv_b,
           ffn_down, final_g):
    bp = x_prompt.shape[0]
    bs = x_sample.shape[0]
    params = [_layer_params(l, norm1_g, w_in, mu_shift, w0, w2, a0, a2, g2, k_k, k_a, r_k, lnx_w, lnx_b,
                            conv_w, conv_b, cln_w, cln_b, w_out, norm2_g, ffn_up, ffn_conv_w,
                            ffn_conv_b, ffn_down) for l in range(DEPTH)]
    vres = [(v0[l].reshape(1, -1), v1[l].astype(BF16), v2[l].astype(BF16)) for l in range(DEPTH - 1)]
    fg = final_g.reshape(1, -1)

    c_all = jnp.concatenate([c_prompt, c_sample], axis=0)
    pad = (-c_all.shape[0]) % 8
    c_all = jnp.concatenate([c_all, jnp.zeros((pad, D_MODEL), F32)], axis=0)
    ada = _ada_call(c_all, ada_w, ada_b).reshape(DEPTH, -1, 6, D_MODEL)
    ada_p, ada_s = ada[:, :bp], ada[:, bp:bp + bs]

    z_wkv = jnp.zeros((DEPTH, bp, RWKV_HEADS, HEAD_SIZE, HEAD_SIZE), F32)
    z_shift = jnp.zeros((DEPTH, bp, 1, RWKV_COLS), F32)
    z_conv = jnp.zeros((DEPTH, bp, CONV_WIDTH - 1, CONV_CH), F32)
    z_ffn = jnp.zeros((DEPTH, bp, FFN_CONV_WIDTH - 1, 2 * D_FF), F32)
    y_p, (wkv_p, shift_p, conv_p, ffn_p) = _trunk(
        x_prompt, ada_p, z_wkv, z_shift, z_conv, z_ffn, params, vres, fg)
    y_s, (wkv_s, shift_s, conv_s, ffn_s) = _trunk(
        x_sample, ada_s, state_wkv, state_shift, state_conv, state_ffn_conv, params, vres, fg)
    return (y_p, y_s, wkv_p, shift_p, conv_p, ffn_p, wkv_s, shift_s, conv_s, ffn_s)
```

```python
import functools

import jax
import jax.numpy as jnp
from jax import lax
from jax.experimental import pallas as pl
from jax.experimental.pallas import tpu as pltpu

D_MODEL = 1024
DEPTH = 2
RWKV_WIDTH = 512
HEAD_SIZE = 64
RWKV_HEADS = RWKV_WIDTH // HEAD_SIZE
DECAY_LORA = 64
ICLR_LORA = 64
GATE_LORA = 128
RWKV_COLS = 3 * RWKV_WIDTH + DECAY_LORA + ICLR_LORA + GATE_LORA
CONV_CH = 512
CONV_WIDTH = 31
IN_COLS = RWKV_COLS + 2 * CONV_CH
D_FF = 2816
FFN_CONV_WIDTH = 3
NORM_EPS = 1e-6
LN_EPS = 1e-5
GN_EPS = 64e-5

MXU_WIDTH = 256
HEADS_PER_GROUP = MXU_WIDTH // HEAD_SIZE
HEAD_GROUPS = RWKV_HEADS // HEADS_PER_GROUP
FFN_SLAB = MXU_WIDTH
FFN_SLABS = D_FF // FFN_SLAB
MAX_CHUNK = 64
MAX_TIME_TILE = 512
CONV_HALO = 32
CONV_ROW_BLOCK = 32
CONV_COPY_ROWS = 64
SUBLANES = 8
VMEM_LIMIT = 56 * 1024 * 1024
INTRA_PASSES = 1
STATE_PASSES = 1
WKV_LOCKSTEP_CHUNKS = 8
FFN_LOOKAHEAD = 2
INTRA_PASSES_PARTS = 1 if INTRA_PASSES == 1 else 2
STATE_PASSES_PARTS = 1 if STATE_PASSES == 1 else 2

F32 = jnp.float32
BF16 = jnp.bfloat16


def _dot(a, b):
    return jnp.dot(a, b, preferred_element_type=F32)


def _dot_nt(a, b):
    return lax.dot_general(a, b, (((1,), (1,)), ((), ())), preferred_element_type=F32)


def _split2(x):
    hi = x.astype(BF16)
    lo = (x - hi.astype(F32)).astype(BF16)
    return hi, lo


def _head_sum(x, seg):
    hi, lo = _split2(x)
    return _dot(hi, seg) + _dot(lo, seg)


def _log2(n):
    assert n & (n - 1) == 0
    return n.bit_length() - 1


def _div(x, n):
    return x >> _log2(n)


def _mod(x, n):
    return x & (n - 1)


def _head_seg_matrix(width):
    row = _div(lax.broadcasted_iota(jnp.int32, (width, width), 0), HEAD_SIZE)
    col = _div(lax.broadcasted_iota(jnp.int32, (width, width), 1), HEAD_SIZE)
    return (row == col).astype(BF16)


def _sigmoid(x):
    return 0.5 * jnp.tanh(0.5 * x) + 0.5


def _silu(x):
    return x * _sigmoid(x)


def _rms_norm(x, g):
    return x * lax.rsqrt(jnp.mean(x * x, axis=-1, keepdims=True) + NORM_EPS) * g


def _shift_rows(x, carry_rows, shift):
    out = pltpu.roll(x, shift, 0)
    head = out[:SUBLANES]
    rows = lax.broadcasted_iota(jnp.int32, head.shape, 0)
    n_carry = carry_rows.shape[0]
    for i in range(shift):
        head = jnp.where(rows == i, carry_rows[n_carry - shift + i:n_carry - shift + i + 1], head)
    return jnp.concatenate([head, out[SUBLANES:]], axis=0)


def _ada_kernel(c_ref, w_ref, b_ref, o_ref):
    c = c_ref[...]
    o_ref[0] = _dot(_silu(c).astype(BF16), w_ref[0].astype(BF16)) + b_ref[0]


def _ada_call(c_all, ada_w, ada_b):
    rows = c_all.shape[0]
    n_tile = 1536
    return pl.pallas_call(
        _ada_kernel,
        grid=(DEPTH, 6 * D_MODEL // n_tile),
        in_specs=[
            pl.BlockSpec((rows, D_MODEL), lambda l, n: (0, 0)),
            pl.BlockSpec((1, D_MODEL, n_tile), lambda l, n: (l, 0, n)),
            pl.BlockSpec((1, 1, n_tile), lambda l, n: (l, 0, n)),
        ],
        out_specs=pl.BlockSpec((1, rows, n_tile), lambda l, n: (l, 0, n)),
        out_shape=jax.ShapeDtypeStruct((DEPTH, rows, 6 * D_MODEL), F32),
        compiler_params=pltpu.CompilerParams(
            dimension_semantics=("arbitrary", "arbitrary"), vmem_limit_bytes=VMEM_LIMIT),
        name="ada_vectors",
    )(c_all, ada_w, ada_b.reshape(DEPTH, 1, 6 * D_MODEL))


def _mix_in_kernel(has_vres, tm, *refs):
    (x_ref, ada_ref, g1_ref, win_ref, sh0_ref, mu_ref, w0_ref, lora_ref, a0_ref, g2_ref,
     kk_ref, ka_ref) = refs[:12]
    refs = refs[12:]
    if has_vres:
        v0_ref, v1_ref, v2_ref, vf_ref = refs[:4]
        refs = refs[4:]
    r_o, lw_o, k_o, v_o, kkn_o, a_o, g_o, glu_o, shift_o, carry = refs

    @pl.when(pl.program_id(1) == 0)
    def _():
        carry[...] = sh0_ref[0]

    x = x_ref[0]
    ada = ada_ref[0]
    h = _rms_norm(x, g1_ref[...]) * (1.0 + ada[1:2]) + ada[0:1]
    proj = _dot(h.astype(BF16), win_ref[...])
    rw = proj[:, :RWKV_COLS]
    cv = proj[:, RWKV_COLS:]
    glu_o[0] = cv[:, :CONV_CH] * _sigmoid(cv[:, CONV_CH:])

    last = rw[tm - 1:tm]
    shift_o[0] = last
    prev = _shift_rows(rw, carry[...], 1)
    carry[...] = last
    xs = rw + (prev - rw) * mu_ref[...]

    w = RWKV_WIDTH
    xr, xk, xv = xs[:, :w], xs[:, w:2 * w], xs[:, 2 * w:3 * w]
    x_lora = xs[:, 3 * w:3 * w + DECAY_LORA + ICLR_LORA]
    xg = xs[:, 3 * w + DECAY_LORA + ICLR_LORA:]
    lane = lax.broadcasted_iota(jnp.int32, x_lora.shape, 1)
    lora_in = jnp.where(lane < DECAY_LORA, jnp.tanh(x_lora), x_lora)
    lora = _dot(lora_in.astype(BF16), lora_ref[...])

    z = -(w0_ref[...] + lora[:, :w])
    softplus = jnp.maximum(z, 0.0) + jnp.log(1.0 + jnp.exp(-jnp.abs(z)))
    lw_o[0] = -jnp.exp(-softplus - 0.5)
    a = _sigmoid(a0_ref[...] + lora[:, w:])
    a_o[0] = a
    g_o[0] = _dot(_sigmoid(xg).astype(BF16), g2_ref[...])

    if has_vres:
        gate = _dot(_dot(xv.astype(BF16), v1_ref[...]).astype(BF16), v2_ref[...])
        v_o[0] = xv + (vf_ref[0] - xv) * _sigmoid(v0_ref[...] + gate)
    else:
        v_o[0] = xv

    kk = xk * kk_ref[...]
    norm = jnp.sqrt(_head_sum(kk * kk, _head_seg_matrix(w)))
    kkn_o[0] = kk / jnp.maximum(norm, 1e-12)
    k_o[0] = xk * (1.0 + (a - 1.0) * ka_ref[...])
    r_o[0] = xr


def _const_spec(shape):
    return pl.BlockSpec(shape, lambda b, t: (0,) * len(shape), pipeline_mode=pl.Buffered(1))


def _mix_in_call(x, ada, shift0, p, vres, v_first, tm):
    bsz, seq, _ = x.shape
    w = RWKV_WIDTH
    tile = lambda width: pl.BlockSpec((1, tm, width), lambda b, t: (b, t, 0))
    per_batch = lambda rows, width: pl.BlockSpec((1, rows, width), lambda b, t: (b, 0, 0))
    args = [x, ada, p["norm1_g"], p["w_in"], shift0, p["mu_shift"], p["w0"], p["lora"], p["a0"],
            p["g2"], p["k_k"], p["k_a"]]
    in_specs = [tile(D_MODEL), per_batch(6, D_MODEL), _const_spec((1, D_MODEL)),
                _const_spec((D_MODEL, IN_COLS)), per_batch(1, RWKV_COLS), _const_spec((1, RWKV_COLS)),
                _const_spec((1, w)), _const_spec((DECAY_LORA + ICLR_LORA, 2 * w)), _const_spec((1, w)),
                _const_spec((GATE_LORA, w)), _const_spec((1, w)), _const_spec((1, w))]
    if vres is not None:
        v0, v1, v2 = vres
        args += [v0, v1, v2, v_first]
        in_specs += [_const_spec((1, w)), _const_spec(v1.shape), _const_spec(v2.shape), tile(w)]
    act = jax.ShapeDtypeStruct((bsz, seq, w), F32)
    return pl.pallas_call(
        functools.partial(_mix_in_kernel, vres is not None, tm),
        grid=(bsz, seq // tm),
        in_specs=in_specs,
        out_specs=[tile(w)] * 8 + [per_batch(1, RWKV_COLS)],
        out_shape=[act] * 8 + [jax.ShapeDtypeStruct((bsz, 1, RWKV_COLS), F32)],
        scratch_shapes=[pltpu.VMEM((1, RWKV_COLS), F32)],
        compiler_params=pltpu.CompilerParams(
            dimension_semantics=("arbitrary", "arbitrary"), vmem_limit_bytes=VMEM_LIMIT),
        name="mix_in",
    )(*args)


def _parts(x, passes):
    return (x.astype(BF16),) if passes == 1 else _split2(x)


def _mm(a_parts, b_parts, dot=_dot):
    out = dot(a_parts[0], b_parts[0])
    if len(a_parts) > 1:
        out = out + dot(a_parts[1], b_parts[0]) + dot(a_parts[0], b_parts[1])
    return out


def _wkv_kernel(tm, chunk, r_ref, lw_ref, k_ref, v_ref, kk_ref, a_ref, g_ref, s0_ref, rk_ref,
                lnw_ref, lnb_ref, out_ref, sn_ref, state, lhs_sc, q_sc, qt_sc, rkv_sc, arb_sc,
                bdec_sc, vk_sc, gl_sc):
    n_heads = HEADS_PER_GROUP
    gw = MXU_WIDTH
    side = n_heads * chunk
    n_chunks = tm // chunk
    lockstep = min(WKV_LOCKSTEP_CHUNKS, n_chunks)

    @pl.when(pl.program_id(1) == 0)
    def _():
        state[...] = s0_ref[0]

    def iota(shape, dim):
        return lax.broadcasted_iota(jnp.int32, shape, dim)

    mask_data = _div(iota((side, gw), 0), chunk) == _div(iota((side, gw), 1), HEAD_SIZE)
    mask_side = _div(iota((side, side), 0), chunk) == _div(iota((side, side), 1), chunk)
    mask_state = _div(iota((gw, gw), 0), HEAD_SIZE) == _div(iota((gw, gw), 1), HEAD_SIZE)
    t_idx = iota((chunk, side), 0)
    s_idx = _mod(iota((chunk, side), 1), chunk)
    strict = s_idx < t_idx
    incl = s_idx <= t_idx
    eye = (s_idx == t_idx).astype(F32)
    row_idx = iota((chunk, RWKV_WIDTH), 0)

    def expand(parts, mask):
        zero = jnp.zeros((), BF16)
        return tuple(jnp.where(mask, jnp.concatenate([q] * n_heads, axis=0), zero) for q in parts)

    def expand_data(x, passes=INTRA_PASSES):
        return expand(_parts(x, passes), mask_data)

    def expand_side(x):
        return expand(_parts(x, INTRA_PASSES), mask_side)

    def intra(x):
        return _parts(x, INTRA_PASSES)

    def on_state(x):
        return _parts(x, STATE_PASSES)

    def lower_left(blk):
        return ((_div(t_idx, 2 * blk) == _div(s_idx, 2 * blk))
                & (_mod(_div(t_idx, blk), 2) == 1) & (_mod(_div(s_idx, blk), 2) == 0))

    def precompute(step_idx, carry):
        chains = []
        for j in range(lockstep):
            c = step_idx * lockstep + j
            rows = pl.ds(pl.multiple_of(c * chunk, chunk), chunk)
            r = r_ref[0, rows, :]
            lw = lw_ref[0, rows, :]
            k = k_ref[0, rows, :]
            v = v_ref[0, rows, :]
            kk = kk_ref[0, rows, :]
            a = a_ref[0, rows, :]

            cum = lw
            step = 1
            while step < chunk:
                cum = cum + jnp.where(row_idx >= step, pltpu.roll(cum, step, 0), 0.0)
                step *= 2
            total = cum[chunk - 1:chunk]
            decay_inv = jnp.exp(-cum)
            decay_out = jnp.exp(total - cum)
            gl_sc[c] = jnp.exp(total)

            bhat = kk * a
            qa = jnp.concatenate([-kk * jnp.exp(cum - lw), r * jnp.exp(cum)], axis=0)
            kb_b = bhat * decay_inv
            kb_k = k * decay_inv
            bdec = bhat * decay_out
            kdec = k * decay_out
            for grp in range(HEAD_GROUPS):
                lanes = slice(grp * gw, (grp + 1) * gw)
                chains.append(dict(c=c, grp=grp, qa=qa[:, lanes], v=v[:, lanes], kb_b=kb_b[:, lanes],
                                   kb_k=kb_k[:, lanes], bdec=bdec[:, lanes], kdec=kdec[:, lanes]))

        for ch in chains:
            qa_p = intra(ch["qa"])
            a_b = _mm(qa_p, expand_data(ch["kb_b"]), _dot_nt)
            a_k = _mm(qa_p, expand_data(ch["kb_k"]), _dot_nt)
            ch["a_ab"] = jnp.where(strict, a_b[:chunk], 0.0)
            ch["a_rb"] = jnp.where(incl, a_b[chunk:], 0.0)
            ch["a_k"] = jnp.where(jnp.concatenate([strict, incl], axis=0), a_k, 0.0)
            ch["tinv"] = eye + jnp.where(lower_left(1), ch["a_ab"], 0.0)
        blk = 2
        while blk < chunk:
            for ch in chains:
                ch["inner"] = _mm(intra(jnp.where(lower_left(blk), ch["a_ab"], 0.0)),
                                  expand_side(ch["tinv"]))
            for ch in chains:
                ch["tinv"] = ch["tinv"] + _mm(intra(ch["tinv"]), expand_side(ch["inner"]))
            blk *= 2

        for ch in chains:
            ch["tinv_p"] = intra(ch["tinv"])
            ch["kv"] = _mm(intra(ch["a_k"]), expand_data(ch["v"]))
            ch["pmat"] = _mm(ch["tinv_p"], expand_data(ch["qa"][:chunk]))
            vk = _mm(on_state(ch["v"].T), on_state(ch["kdec"]))
            vk_sc[ch["c"], ch["grp"]] = jnp.where(mask_state, vk, 0.0)
        for ch in chains:
            c, grp = ch["c"], ch["grp"]
            q = _mm(ch["tinv_p"], expand_data(ch["kv"][:chunk]))
            q_sc[c, grp] = q
            qt_sc[c, grp] = q.T
            rkv_sc[c, grp] = ch["kv"][chunk:]
            lhs = jnp.concatenate([ch["pmat"], ch["qa"][chunk:]], axis=0)
            for i, part in enumerate(on_state(lhs)):
                lhs_sc[c, grp, i] = part
            for i, part in enumerate(on_state(ch["bdec"])):
                bdec_sc[c, grp, i] = part
            for i, part in enumerate(intra(ch["a_rb"])):
                arb_sc[c, grp, i] = part
        return carry

    lax.fori_loop(0, n_chunks // lockstep, precompute, 0)

    def lhs_parts(c, grp, n_rows):
        return tuple(lhs_sc[c, grp, i, 0:n_rows, :] for i in range(STATE_PASSES_PARTS))

    def outputs_first_half(c, s_parts):
        return [_mm(lhs_parts(c, grp, 2 * chunk), s_parts[grp], _dot_nt)
                for grp in range(HEAD_GROUPS)]

    def outputs_second_half(c, xu):
        for grp in range(HEAD_GROUPS):
            lanes = slice(grp * gw, (grp + 1) * gw)
            u = xu[grp][:chunk] + q_sc[c, grp]
            arb = tuple(arb_sc[c, grp, i] for i in range(INTRA_PASSES_PARTS))
            out_ref[0, c * chunk:(c + 1) * chunk, lanes] = (
                xu[grp][chunk:] + rkv_sc[c, grp] + _mm(arb, expand_data(u)))

    late = None
    for c in range(n_chunks):
        s_old = [state[grp] for grp in range(HEAD_GROUPS)]
        s_parts = [on_state(s) for s in s_old]
        u_t = [_mm(s_parts[grp], lhs_parts(c, grp, chunk), _dot_nt) + qt_sc[c, grp]
               for grp in range(HEAD_GROUPS)]
        xu_late = outputs_first_half(*late) if late is not None else None
        decay_all = gl_sc[c]
        for grp in range(HEAD_GROUPS):
            lanes = slice(grp * gw, (grp + 1) * gw)
            bd = tuple(bdec_sc[c, grp, i] for i in range(STATE_PASSES_PARTS))
            upd = _mm(on_state(u_t[grp]), bd)
            state[grp] = (s_old[grp] * decay_all[:, lanes] + vk_sc[c, grp]
                          + jnp.where(mask_state, upd, 0.0))
        if late is not None:
            outputs_second_half(late[0], xu_late)
        late = (c, s_parts)
    outputs_second_half(late[0], outputs_first_half(*late))
    sn_ref[0] = state[...]

    seg = _head_seg_matrix(gw)

    def head_sum(x):
        hi, lo = _split2(jnp.concatenate([x[:, :gw], x[:, gw:]], axis=0))
        s = _dot(jnp.concatenate([hi, lo], axis=0), seg)
        s = s[:2 * tm] + s[2 * tm:]
        return jnp.concatenate([s[:tm], s[tm:]], axis=1)

    y = out_ref[0]
    inv_n = 1.0 / HEAD_SIZE
    cen = y - head_sum(y) * inv_n
    var = head_sum(cen * cen) * inv_n
    yn = cen * lax.rsqrt(var + GN_EPS) * lnw_ref[...] + lnb_ref[...]
    bonus = head_sum(r_ref[0] * k_ref[0] * rk_ref[...]) * v_ref[0]
    out_ref[0] = (yn + bonus) * g_ref[0]


def _wkv_call(r, lw, k, v, kk, a, g, s0_bd, p, tm, chunk):
    bsz, seq, w = r.shape
    tile = pl.BlockSpec((1, tm, w), lambda b, t: (b, t, 0))
    st = pl.BlockSpec((1, HEAD_GROUPS, MXU_WIDTH, MXU_WIDTH), lambda b, t: (b, 0, 0, 0))
    vec = _const_spec((1, w))
    gw = MXU_WIDTH
    side = HEADS_PER_GROUP * chunk
    n_chunks = tm // chunk
    return pl.pallas_call(
        functools.partial(_wkv_kernel, tm, chunk),
        grid=(bsz, seq // tm),
        in_specs=[tile] * 7 + [st, vec, vec, vec],
        out_specs=[tile, st],
        out_shape=[jax.ShapeDtypeStruct((bsz, seq, w), F32),
                   jax.ShapeDtypeStruct((bsz, HEAD_GROUPS, MXU_WIDTH, MXU_WIDTH), F32)],
        scratch_shapes=[
            pltpu.VMEM((HEAD_GROUPS, gw, gw), F32),
            pltpu.VMEM((n_chunks, HEAD_GROUPS, STATE_PASSES_PARTS, 2 * chunk, gw), BF16),
            pltpu.VMEM((n_chunks, HEAD_GROUPS, chunk, gw), F32),
            pltpu.VMEM((n_chunks, HEAD_GROUPS, gw, chunk), F32),
            pltpu.VMEM((n_chunks, HEAD_GROUPS, chunk, gw), F32),
            pltpu.VMEM((n_chunks, HEAD_GROUPS, INTRA_PASSES_PARTS, chunk, side), BF16),
            pltpu.VMEM((n_chunks, HEAD_GROUPS, STATE_PASSES_PARTS, chunk, gw), BF16),
            pltpu.VMEM((n_chunks, HEAD_GROUPS, gw, gw), F32),
            pltpu.VMEM((n_chunks, 1, w), F32),
        ],
        compiler_params=pltpu.CompilerParams(
            dimension_semantics=("arbitrary", "arbitrary"), vmem_limit_bytes=VMEM_LIMIT),
        name="wkv7_chunked",
    )(r, lw, k, v, kk, a, g, s0_bd, p["r_k"], p["lnx_w"], p["lnx_b"])


def _state_to_blockdiag(s):
    bsz = s.shape[0]
    s = s.reshape(bsz, HEAD_GROUPS, HEADS_PER_GROUP, HEAD_SIZE, HEAD_SIZE)
    eye = jnp.eye(HEADS_PER_GROUP, dtype=s.dtype)
    out = jnp.einsum("bgivk,ij->bgivjk", s, eye)
    return out.reshape(bsz, HEAD_GROUPS, MXU_WIDTH, MXU_WIDTH)


def _blockdiag_to_state(s_bd):
    bsz = s_bd.shape[0]
    s = s_bd.reshape(bsz, HEAD_GROUPS, HEADS_PER_GROUP, HEAD_SIZE, HEADS_PER_GROUP, HEAD_SIZE)
    idx = jnp.arange(HEADS_PER_GROUP)
    s = s[:, :, idx, :, idx, :]
    s = jnp.moveaxis(s, 0, 2)
    return s.reshape(bsz, RWKV_HEADS, HEAD_SIZE, HEAD_SIZE)


def _conv_kernel(tm, glu_ref, past_ref, w_ref, b_ref, lnw_ref, lnb_ref, out_ref, new_ref, buf,
                 shifted):
    hist = CONV_WIDTH - 1
    pad = CONV_HALO - hist
    span = CONV_HALO + tm

    @pl.when(pl.program_id(1) == 0)
    def _():
        buf[0:pad, :] = jnp.zeros((pad, CONV_CH), F32)
        buf[pad:CONV_HALO, :] = past_ref[0]

    buf[CONV_HALO:span, :] = glu_ref[0]
    n_shifted = span - SUBLANES
    for s in range(1, SUBLANES):
        for r0 in range(0, n_shifted, CONV_COPY_ROWS):
            n = min(CONV_COPY_ROWS, n_shifted - r0)
            shifted[s - 1, r0:r0 + n, :] = buf[r0 + s:r0 + s + n, :]

    blk = min(CONV_ROW_BLOCK, tm)
    for rb in range(tm // blk):
        acc = jnp.zeros((blk, CONV_CH), F32) + b_ref[...]
        for j in range(CONV_WIDTH):
            q, s = divmod(j + pad, SUBLANES)
            r0 = rb * blk + q * SUBLANES
            rows = buf[r0:r0 + blk, :] if s == 0 else shifted[s - 1, r0:r0 + blk, :]
            acc = acc + w_ref[j:j + 1, :] * rows
        mean = jnp.mean(acc, axis=-1, keepdims=True)
        cen = acc - mean
        var = jnp.mean(cen * cen, axis=-1, keepdims=True)
        yn = cen * lax.rsqrt(var + LN_EPS) * lnw_ref[...] + lnb_ref[...]
        out_ref[0, rb * blk:(rb + 1) * blk, :] = _silu(yn)
    new_ref[0] = buf[tm + pad:tm + CONV_HALO, :]
    buf[0:CONV_HALO, :] = buf[tm:tm + CONV_HALO, :]


def _conv_call(glu, past, p, tm):
    bsz, seq, ch = glu.shape
    hist = CONV_WIDTH - 1
    tile = pl.BlockSpec((1, tm, ch), lambda b, t: (b, t, 0))
    st = pl.BlockSpec((1, hist, ch), lambda b, t: (b, 0, 0))
    vec = _const_spec((1, ch))
    return pl.pallas_call(
        functools.partial(_conv_kernel, tm),
        grid=(bsz, seq // tm),
        in_specs=[tile, st, _const_spec((CONV_WIDTH, ch)), vec, vec, vec],
        out_specs=[tile, st],
        out_shape=[jax.ShapeDtypeStruct((bsz, seq, ch), F32),
                   jax.ShapeDtypeStruct((bsz, hist, ch), F32)],
        scratch_shapes=[pltpu.VMEM((CONV_HALO + tm, ch), F32),
                        pltpu.VMEM((SUBLANES - 1, CONV_HALO + tm - SUBLANES, ch), F32)],
        compiler_params=pltpu.CompilerParams(
            dimension_semantics=("arbitrary", "arbitrary"), vmem_limit_bytes=VMEM_LIMIT),
        name="conformer_conv",
    )(glu, past, p["conv_w"], p["conv_b"], p["cln_w"], p["cln_b"])


def _ffn_kernel(final, tm, x_ref, a_ref, b_ref, ada_ref, woa_ref, wob_ref, g2_ref, up_ref, cw_ref,
                cb_ref, down_ref, past_ref, fg_ref, out_ref, new_ref, h2_sc, act_sc, carry):
    @pl.when(pl.program_id(1) == 0)
    def _():
        carry[...] = past_ref[0]

    ada = ada_ref[0]
    mix = _dot(a_ref[0].astype(BF16), woa_ref[...]) + _dot(b_ref[0].astype(BF16), wob_ref[...])
    x1 = x_ref[0] + ada[2:3] * mix
    h2 = _rms_norm(x1, g2_ref[...]) * (1.0 + ada[4:5]) + ada[3:4]
    h2_sc[...] = h2.astype(BF16)

    def cols(j):
        return slice(j * FFN_SLAB, (j + 1) * FFN_SLAB)

    def up_proj(j):
        return (_dot(h2_sc[...], up_ref[:, cols(j)]),
                _dot(h2_sc[...], up_ref[:, cols(j + FFN_SLABS)]))

    def causal_conv(u, j):
        past = carry[:, cols(j)]
        cw = cw_ref[:, cols(j)]
        y = (cw[0:1] * _shift_rows(u, past, 2) + cw[1:2] * _shift_rows(u, past, 1)
             + cw[2:3] * u + cb_ref[:, cols(j)])
        carry[:, cols(j)] = u[tm - (FFN_CONV_WIDTH - 1):tm]
        return y

    ahead = [up_proj(j) for j in range(min(FFN_LOOKAHEAD, FFN_SLABS))]
    for j in range(FFN_SLABS):
        u_gate, u_val = ahead.pop(0)
        if j + FFN_LOOKAHEAD < FFN_SLABS:
            ahead.append(up_proj(j + FFN_LOOKAHEAD))
        act = _silu(causal_conv(u_gate, j)) * causal_conv(u_val, j + FFN_SLABS)
        act_sc[:, cols(j)] = act.astype(BF16)
    x2 = x1 + ada[5:6] * _dot(act_sc[...], down_ref[...])
    out_ref[0] = _rms_norm(x2, fg_ref[...]) if final else x2
    new_ref[0] = carry[...]


def _ffn_call(x, a_out, b_out, ada, past, p, final_g, final, tm):
    bsz, seq, _ = x.shape
    hist = FFN_CONV_WIDTH - 1
    tile = lambda width: pl.BlockSpec((1, tm, width), lambda b, t: (b, t, 0))
    st = pl.BlockSpec((1, hist, 2 * D_FF), lambda b, t: (b, 0, 0))
    half = RWKV_WIDTH
    return pl.pallas_call(
        functools.partial(_ffn_kernel, final, tm),
        grid=(bsz, seq // tm),
        in_specs=[tile(D_MODEL), tile(half), tile(CONV_CH),
                  pl.BlockSpec((1, 6, D_MODEL), lambda b, t: (b, 0, 0)),
                  _const_spec((half, D_MODEL)), _const_spec((CONV_CH, D_MODEL)),
                  _const_spec((1, D_MODEL)),
                  _const_spec((D_MODEL, 2 * D_FF)),
                  _const_spec((FFN_CONV_WIDTH, 2 * D_FF)),
                  _const_spec((1, 2 * D_FF)),
                  _const_spec((D_FF, D_MODEL)),
                  st, _const_spec((1, D_MODEL))],
        out_specs=[tile(D_MODEL), st],
        out_shape=[jax.ShapeDtypeStruct((bsz, seq, D_MODEL), F32),
                   jax.ShapeDtypeStruct((bsz, hist, 2 * D_FF), F32)],
        scratch_shapes=[pltpu.VMEM((tm, D_MODEL), BF16), pltpu.VMEM((tm, D_FF), BF16),
                        pltpu.VMEM((hist, 2 * D_FF), F32)],
        compiler_params=pltpu.CompilerParams(
            dimension_semantics=("arbitrary", "arbitrary"), vmem_limit_bytes=VMEM_LIMIT),
        name="mix_out_ffn",
    )(x, a_out, b_out, ada, p["w_out_a"], p["w_out_b"], p["norm2_g"], p["ffn_up"], p["ffn_conv_w"],
      p["ffn_conv_b"], p["ffn_down"], past, final_g)


def _layer_params(l, norm1_g, w_in, mu_shift, w0, w2, a0, a2, g2, k_k, k_a, r_k, lnx_w, lnx_b, conv_w,
                  conv_b, cln_w, cln_b, w_out, norm2_g, ffn_up, ffn_conv_w, ffn_conv_b, ffn_down):
    row = lambda t: t[l].reshape(1, -1)
    zeros = jnp.zeros((DECAY_LORA, RWKV_WIDTH), F32)
    lora = jnp.concatenate([jnp.concatenate([w2[l], zeros], axis=1),
                            jnp.concatenate([zeros, a2[l]], axis=1)], axis=0)
    return dict(
        norm1_g=row(norm1_g), w_in=w_in[l].astype(BF16), mu_shift=row(mu_shift), w0=row(w0),
        lora=lora.astype(BF16), a0=row(a0), g2=g2[l].astype(BF16), k_k=row(k_k), k_a=row(k_a),
        r_k=row(r_k), lnx_w=row(lnx_w), lnx_b=row(lnx_b), conv_w=conv_w[l], conv_b=row(conv_b),
        cln_w=row(cln_w), cln_b=row(cln_b),
        w_out_a=w_out[l, :RWKV_WIDTH].astype(BF16), w_out_b=w_out[l, RWKV_WIDTH:].astype(BF16),
        norm2_g=row(norm2_g),
        ffn_up=ffn_up[l].astype(BF16), ffn_conv_w=ffn_conv_w[l], ffn_conv_b=row(ffn_conv_b),
        ffn_down=ffn_down[l].astype(BF16),
    )


def _trunk(x, ada, st_wkv, st_shift, st_conv, st_ffn, params, vres, final_g):
    bsz, seq, _ = x.shape
    tm = min(MAX_TIME_TILE, seq)
    chunk = min(MAX_CHUNK, seq)
    v_first = None
    new = ([], [], [], [])
    for l in range(DEPTH):
        p = params[l]
        r, lw, k, v, kk, a, g, glu, n_shift = _mix_in_call(
            x, ada[l], st_shift[l], p, None if l == 0 else vres[l - 1], v_first, tm)
        if l == 0:
            v_first = v
        a_out, s_bd = _wkv_call(r, lw, k, v, kk, a, g, _state_to_blockdiag(st_wkv[l]), p, tm, chunk)
        b_out, n_conv = _conv_call(glu, st_conv[l], p, tm)
        x, n_ffn = _ffn_call(x, a_out, b_out, ada[l], st_ffn[l], p, final_g, l == DEPTH - 1, tm)
        new[0].append(_blockdiag_to_state(s_bd))
        new[1].append(n_shift)
        new[2].append(n_conv)
        new[3].append(n_ffn)
    return x, [jnp.stack(t) for t in new]


def kernel(x_prompt, x_sample, c_prompt, c_sample, state_wkv, state_shift, state_conv, state_ffn_conv,
           norm1_g, ada_w, ada_b, w_in, mu_shift, w0, w2, a0, a2, g2, k_k, k_a, r_k, v0, v1, v2,
           lnx_w, lnx_b, conv_w, conv_b, cln_w, cln_b, w_out, norm2_g, ffn_up, ffn_conv_w, ffn_conv_b,
           ffn_down, final_g):
    bp = x_prompt.shape[0]
    bs = x_sample.shape[0]
    params = [_layer_params(l, norm1_g, w_in, mu_shift, w0, w2, a0, a2, g2, k_k, k_a, r_k, lnx_w, lnx_b,
                            conv_w, conv_b, cln_w, cln_b, w_out, norm2_g, ffn_up, ffn_conv_w,
                            ffn_conv_b, ffn_down) for l in range(DEPTH)]
    vres = [(v0[l].reshape(1, -1), v1[l].astype(BF16), v2[l].astype(BF16)) for l in range(DEPTH - 1)]
    fg = final_g.reshape(1, -1)

    c_all = jnp.concatenate([c_prompt, c_sample], axis=0)
    pad = (-c_all.shape[0]) % 8
    c_all = jnp.concatenate([c_all, jnp.zeros((pad, D_MODEL), F32)], axis=0)
    ada = _ada_call(c_all, ada_w, ada_b).reshape(DEPTH, -1, 6, D_MODEL)
    ada_p, ada_s = ada[:, :bp], ada[:, bp:bp + bs]

    z_wkv = jnp.zeros((DEPTH, bp, RWKV_HEADS, HEAD_SIZE, HEAD_SIZE), F32)
    z_shift = jnp.zeros((DEPTH, bp, 1, RWKV_COLS), F32)
    z_conv = jnp.zeros((DEPTH, bp, CONV_WIDTH - 1, CONV_CH), F32)
    z_ffn = jnp.zeros((DEPTH, bp, FFN_CONV_WIDTH - 1, 2 * D_FF), F32)
    y_p, (wkv_p, shift_p, conv_p, ffn_p) = _trunk(
        x_prompt, ada_p, z_wkv, z_shift, z_conv, z_ffn, params, vres, fg)
    y_s, (wkv_s, shift_s, conv_s, ffn_s) = _trunk(
        x_sample, ada_s, state_wkv, state_shift, state_conv, state_ffn_conv, params, vres, fg)
    return (y_p, y_s, wkv_p, shift_p, conv_p, ffn_p, wkv_s, shift_s, conv_s, ffn_s)
```
